```python
import math
import jax, jax.numpy as jnp
from jax import lax
import numpy as np

D_MODEL = 1024
BATCH = 8
SEQ = 2048
DEPTH = 4

N_Q_HEADS = 8
N_KV_HEADS = 2
HEAD_DIM = 64
Q_GROUP = N_Q_HEADS // N_KV_HEADS
WINDOW = 128
BLOCK = 128
ROPE_THETA = 500000.0
ROT_DIM = HEAD_DIM // 4
ATTN_WIDTH = N_Q_HEADS * HEAD_DIM
KV_WIDTH = N_KV_HEADS * HEAD_DIM
NEG_INF = -1e30
CONV_WIDTH = D_MODEL // 2
CONV_K = 3
SSM_WIDTH = D_MODEL // 2
SSM_GROUP = 16
SSM_GROUPS = SSM_WIDTH // SSM_GROUP
SSM_STATE = 64
DT_MIN = 1e-3
DT_MAX = 1e-1
N_BRANCH = 3
GATE_WIDTH = N_BRANCH * D_MODEL
FFN_HIDDEN = -(-8 * D_MODEL // (3 * 256)) * 256
NORM_EPS = 1e-6

IN_SIZES = (ATTN_WIDTH, KV_WIDTH, KV_WIDTH, CONV_WIDTH, CONV_WIDTH, CONV_WIDTH, SSM_WIDTH, GATE_WIDTH)
IN_COLS = sum(IN_SIZES)
IN_SPLITS = tuple(int(v) for v in np.cumsum(IN_SIZES)[:-1])

kernel_name = "hybrid_gated_swa_conv_s5_block"


def rmsnorm(x, g):
    xf = x.astype(jnp.float32)
    y = xf * lax.rsqrt(jnp.mean(xf * xf, axis=-1, keepdims=True) + NORM_EPS)
    return (y * g.astype(jnp.float32)).astype(x.dtype)


def rope_tables(seq_len):
    pos = jnp.arange(seq_len, dtype=jnp.float32)
    inv_freq = ROPE_THETA ** (-jnp.arange(0, ROT_DIM, 2, dtype=jnp.float32) / ROT_DIM)
    ang = pos[:, None] * inv_freq[None, :]
    return jnp.cos(ang), jnp.sin(ang)


def partial_rope(t, cos, sin):
    half = ROT_DIM // 2
    tf = t.astype(jnp.float32)
    t1, t2, rest = tf[..., :half], tf[..., half:ROT_DIM], tf[..., ROT_DIM:]
    c = cos[None, :, None, :]
    s = sin[None, :, None, :]
    out = jnp.concatenate([t1 * c - t2 * s, t2 * c + t1 * s, rest], axis=-1)
    return out.astype(t.dtype)


def sliding_window_attention(q, k, v, sinks):
    b, l = q.shape[0], q.shape[1]
    nb = l // BLOCK
    qb = q.reshape(b, nb, BLOCK, N_KV_HEADS, Q_GROUP, HEAD_DIM).astype(jnp.float32)

    def band(t):
        tp = jnp.pad(t, ((0, 0), (BLOCK, 0), (0, 0), (0, 0)))
        tp = tp.reshape(b, nb + 1, BLOCK, N_KV_HEADS, HEAD_DIM)
        return jnp.concatenate([tp[:, :-1], tp[:, 1:]], axis=2).astype(jnp.float32)

    kb, vb = band(k), band(v)
    s = jnp.einsum("bnqkgd,bnskd->bnkgqs", qb, kb) * (HEAD_DIM ** -0.5)
    qi = jnp.arange(BLOCK)[:, None]
    kj = jnp.arange(2 * BLOCK)[None, :]
    delta = qi + BLOCK - kj
    band_ok = (delta >= 0) & (delta < WINDOW)
    kpos = jnp.arange(nb)[:, None] * BLOCK - BLOCK + kj
    mask = band_ok[None, :, :] & (kpos >= 0)[:, None, :]
    s = jnp.where(mask[None, :, None, None, :, :], s, NEG_INF)
    sink = sinks.astype(jnp.float32).reshape(N_KV_HEADS, Q_GROUP)[None, None, :, :, None, None]
    m = jnp.maximum(jnp.max(s, axis=-1, keepdims=True), sink)
    p = jnp.exp(s - m)
    denom = jnp.sum(p, axis=-1, keepdims=True) + jnp.exp(sink - m)
    o = jnp.einsum("bnkgqs,bnskd->bnqkgd", p / denom, vb)
    return o.reshape(b, l, ATTN_WIDTH).astype(q.dtype)


def short_conv(z, w):
    l = z.shape[1]
    zp = jnp.pad(z, ((0, 0), (CONV_K - 1, 0), (0, 0)))
    y = w[0] * zp[:, 0:l]
    for j in range(1, CONV_K):
        y = y + w[j] * zp[:, j:j + l]
    return y


def s5_ssm(u, a_re, a_im, b_re, b_im, c_re, c_im, d, log_dt):
    bsz, l = u.shape[0], u.shape[1]
    uf = u.astype(jnp.float32).reshape(bsz, l, SSM_GROUPS, SSM_GROUP)
    lam = lax.complex(a_re.astype(jnp.float32), a_im.astype(jnp.float32))
    dt = jnp.exp(log_dt.astype(jnp.float32))[:, None]
    lam_bar = jnp.exp(lam * dt)
    b_c = lax.complex(b_re.astype(jnp.float32), b_im.astype(jnp.float32))
    b_bar = ((lam_bar - 1.0) / lam)[..., None] * b_c
    bu = jnp.einsum("blgh,gph->blgp", uf.astype(jnp.complex64), b_bar)
    a_elems = jnp.broadcast_to(lam_bar, bu.shape)

    def combine(e1, e2):
        a1, x1 = e1
        a2, x2 = e2
        return a1 * a2, a2 * x1 + x2

    _, states = lax.associative_scan(combine, (a_elems, bu), axis=1)
    c_c = lax.complex(c_re.astype(jnp.float32), c_im.astype(jnp.float32))
    y = jnp.einsum("blgp,ghp->blgh", states, c_c).real
    y = y + d.astype(jnp.float32).reshape(SSM_GROUPS, SSM_GROUP) * uf
    return y.reshape(bsz, l, SSM_WIDTH).astype(u.dtype)


def setup_inputs(seed: int = 0) -> dict:
    key = jax.random.key(seed)
    ks = jax.random.split(key, 24)
    L = DEPTH

    def nrm(k, shape, fan_in):
        return jax.random.normal(k, shape, jnp.float32) * (fan_in ** -0.5)

    x = jax.random.normal(ks[0], (BATCH, SEQ, D_MODEL), jnp.float32)
    norm_mix = 1.0 + 0.02 * jax.random.normal(ks[1], (L, D_MODEL), jnp.float32)
    w_in = nrm(ks[2], (L, D_MODEL, IN_COLS), D_MODEL)
    b_gate = 0.02 * jax.random.normal(ks[3], (L, GATE_WIDTH), jnp.float32)
    attn_sinks = 0.5 * jax.random.normal(ks[4], (L, N_Q_HEADS), jnp.float32)
    w_attn_o = nrm(ks[5], (L, ATTN_WIDTH, D_MODEL), ATTN_WIDTH)
    conv_w = nrm(ks[6], (L, CONV_K, CONV_WIDTH), CONV_K)
    w_conv_o = nrm(ks[7], (L, CONV_WIDTH, D_MODEL), CONV_WIDTH)
    ssm_a_re = -0.5 + 0.01 * jax.random.normal(ks[8], (L, SSM_GROUPS, SSM_STATE), jnp.float32)
    ssm_a_im = (math.pi * jnp.arange(SSM_STATE, dtype=jnp.float32))[None, None, :] \
        + 0.01 * jax.random.normal(ks[9], (L, SSM_GROUPS, SSM_STATE), jnp.float32)
    ssm_b_re = nrm(ks[10], (L, SSM_GROUPS, SSM_STATE, SSM_GROUP), 2 * SSM_GROUP)
    ssm_b_im = nrm(ks[11], (L, SSM_GROUPS, SSM_STATE, SSM_GROUP), 2 * SSM_GROUP)
    ssm_c_re = nrm(ks[12], (L, SSM_GROUPS, SSM_GROUP, SSM_STATE), 2 * SSM_STATE)
    ssm_c_im = nrm(ks[13], (L, SSM_GROUPS, SSM_GROUP, SSM_STATE), 2 * SSM_STATE)
    ssm_d = jax.random.normal(ks[14], (L, SSM_WIDTH), jnp.float32)
    ssm_log_dt = jax.random.uniform(ks[15], (L, SSM_GROUPS), jnp.float32,
                                    minval=math.log(DT_MIN), maxval=math.log(DT_MAX))
    w_ssm_glu = nrm(ks[16], (L, SSM_WIDTH, SSM_WIDTH), SSM_WIDTH)
    w_ssm_o = nrm(ks[17], (L, SSM_WIDTH, D_MODEL), SSM_WIDTH)
    w_mix_o = nrm(ks[18], (L, D_MODEL, D_MODEL), D_MODEL)
    norm_ffn = 1.0 + 0.02 * jax.random.normal(ks[19], (L, D_MODEL), jnp.float32)
    w_ffn_in = nrm(ks[20], (L, D_MODEL, 2 * FFN_HIDDEN), D_MODEL)
    w_ffn_out = nrm(ks[21], (L, FFN_HIDDEN, D_MODEL), FFN_HIDDEN)
    norm_final = 1.0 + 0.02 * jax.random.normal(ks[22], (D_MODEL,), jnp.float32)
    return {"x": x, "norm_mix": norm_mix, "w_in": w_in, "b_gate": b_gate,
            "attn_sinks": attn_sinks, "w_attn_o": w_attn_o, "conv_w": conv_w, "w_conv_o": w_conv_o,
            "ssm_a_re": ssm_a_re, "ssm_a_im": ssm_a_im, "ssm_b_re": ssm_b_re, "ssm_b_im": ssm_b_im,
            "ssm_c_re": ssm_c_re, "ssm_c_im": ssm_c_im, "ssm_d": ssm_d, "ssm_log_dt": ssm_log_dt,
            "w_ssm_glu": w_ssm_glu, "w_ssm_o": w_ssm_o, "w_mix_o": w_mix_o, "norm_ffn": norm_ffn,
            "w_ffn_in": w_ffn_in, "w_ffn_out": w_ffn_out, "norm_final": norm_final}


def reference(x, norm_mix, w_in, b_gate, attn_sinks, w_attn_o, conv_w, w_conv_o,
              ssm_a_re, ssm_a_im, ssm_b_re, ssm_b_im, ssm_c_re, ssm_c_im, ssm_d, ssm_log_dt,
              w_ssm_glu, w_ssm_o, w_mix_o, norm_ffn, w_ffn_in, w_ffn_out, norm_final):
    b, l = x.shape[0], x.shape[1]
    cos, sin = rope_tables(l)
    for i in range(DEPTH):
        h = rmsnorm(x, norm_mix[i])
        proj = h @ w_in[i]
        q, k, v, cb, cc, cx, u, g = jnp.split(proj, IN_SPLITS, axis=-1)
        q = partial_rope(q.reshape(b, l, N_Q_HEADS, HEAD_DIM), cos, sin)
        k = partial_rope(k.reshape(b, l, N_KV_HEADS, HEAD_DIM), cos, sin)
        v = v.reshape(b, l, N_KV_HEADS, HEAD_DIM)
        y_attn = sliding_window_attention(q, k, v, attn_sinks[i]) @ w_attn_o[i]
        y_conv = (cb * short_conv(cc * cx, conv_w[i])) @ w_conv_o[i]
        ys = jax.nn.gelu(s5_ssm(u, ssm_a_re[i], ssm_a_im[i], ssm_b_re[i], ssm_b_im[i],
                                ssm_c_re[i], ssm_c_im[i], ssm_d[i], ssm_log_dt[i]))
        y_ssm = (ys * jax.nn.sigmoid(ys @ w_ssm_glu[i])) @ w_ssm_o[i]
        gates = jax.nn.sigmoid(g + b_gate[i]).reshape(b, l, N_BRANCH, D_MODEL)
        merged = gates[:, :, 0] * y_attn + gates[:, :, 1] * y_conv + gates[:, :, 2] * y_ssm
        x = x + merged @ w_mix_o[i]
        h = rmsnorm(x, norm_ffn[i])
        gt, up = jnp.split(h @ w_ffn_in[i], 2, axis=-1)
        x = x + (jax.nn.silu(gt) * up) @ w_ffn_out[i]
    return rmsnorm(x, norm_final)
```

```python
import functools
import math

import jax
import jax.numpy as jnp
import numpy as np
from jax import lax
from jax.experimental import pallas as pl
from jax.experimental.pallas import tpu as pltpu

D_MODEL = 1024
N_Q_HEADS = 8
N_KV_HEADS = 2
HEAD_DIM = 64
Q_GROUP = N_Q_HEADS // N_KV_HEADS
WINDOW = 128
ROPE_THETA = 500000.0
ROT_DIM = HEAD_DIM // 4
ATTN_WIDTH = N_Q_HEADS * HEAD_DIM
KV_WIDTH = N_KV_HEADS * HEAD_DIM
NEG_INF = -1e30
CONV_WIDTH = D_MODEL // 2
CONV_K = 3
SSM_WIDTH = D_MODEL // 2
SSM_GROUP = 16
SSM_GROUPS = SSM_WIDTH // SSM_GROUP
SSM_STATE = 64
GATE_WIDTH = 3 * D_MODEL
FFN_HIDDEN = 2816
NORM_EPS = 1e-6

_Q0 = 0
_KV0 = _Q0 + ATTN_WIDTH
_CB0 = _KV0 + 2 * KV_WIDTH
_CC0 = _CB0 + CONV_WIDTH
_CX0 = _CC0 + CONV_WIDTH
_U0 = _CX0 + CONV_WIDTH
_G0 = _U0 + SSM_WIDTH
IN_COLS = _G0 + GATE_WIDTH

LANES = 128
SUBLANES = 8
SSM_CHUNK = 8
SSM_SLABS = SSM_WIDTH // LANES
SLAB_GROUPS = LANES // SSM_GROUP
SLAB_STATE = SLAB_GROUPS * SSM_STATE
TOKEN_TILE = 256
SSM_ROW_BLOCK = 512
VMEM_LIMIT = 56 * 1024 * 1024

_BF16 = jnp.bfloat16
_F32 = jnp.float32


def _dot(a, b):
    return jnp.dot(a, b, preferred_element_type=_F32)


def _sigmoid(x):
    return 0.5 * jnp.tanh(0.5 * x) + 0.5


def _rmsnorm(x, w):
    ms = jnp.mean(x * x, axis=-1, keepdims=True)
    return x * lax.rsqrt(ms + NORM_EPS) * w


def _const_spec(shape):
    return pl.BlockSpec(shape, lambda *_: (0,) * len(shape), pipeline_mode=pl.Buffered(1))


def _mixer_in_kernel(sinks_ref, x_ref, nw_ref, win_ref, bg_ref, cos_ref, sa_ref, sb_ref, cw_ref,
                     wao_ref, wco_ref, part_ref, g2_ref, u_ref,
                     kv_scr, z_scr, u_scr, attn_scr):
    tm = x_ref.shape[1]
    li = pl.program_id(1)

    @pl.when(li == 0)
    def _():
        kv_scr[:, 0:WINDOW, :] = jnp.zeros((8, WINDOW, LANES), _BF16)
        z_scr[0:SUBLANES, :] = jnp.zeros((SUBLANES, CONV_WIDTH), _F32)

    x = x_ref[0]
    h = _rmsnorm(x, nw_ref[...]).astype(_BF16)

    cos = cos_ref[...]
    sa = sa_ref[...]
    sb = sb_ref[...]

    def rope(t):
        return (t * cos + pltpu.roll(t, LANES - ROT_DIM // 2, axis=1) * sa
                + pltpu.roll(t, ROT_DIM // 2, axis=1) * sb)

    kv = _dot(h, win_ref[:, _KV0:_KV0 + 2 * KV_WIDTH])
    lane = lax.broadcasted_iota(jnp.int32, (tm, LANES), 1)
    lo = lane < HEAD_DIM
    for kind in range(2):
        t = kv[:, kind * LANES:(kind + 1) * LANES]
        if kind == 0:
            t = rope(t)
        t_sw = pltpu.roll(t, HEAD_DIM, axis=1)
        zero = jnp.zeros_like(t)
        variants = (jnp.where(lo, t, zero), jnp.where(lo, zero, t_sw),
                    jnp.where(lo, t_sw, zero), jnp.where(lo, zero, t))
        for n, val in enumerate(variants):
            kv_scr[kind * 4 + n, WINDOW:WINDOW + tm, :] = val.astype(_BF16)

    q = _dot(h, win_ref[:, _Q0:_Q0 + ATTN_WIDTH])
    row = lax.broadcasted_iota(jnp.int32, (WINDOW, 2 * WINDOW), 0)
    col = lax.broadcasted_iota(jnp.int32, (WINDOW, 2 * WINDOW), 1)
    band_ok = (col > row) & (col <= row + WINDOW)
    first_ok = band_ok & (col >= jnp.where(li > 0, 0, WINDOW))
    for m in range(N_Q_HEADS // 2):
        qp = (rope(q[:, m * LANES:(m + 1) * LANES]) * (HEAD_DIM ** -0.5)).astype(_BF16)
        j = (2 * m) // Q_GROUP
        for r in range(tm // WINDOW):
            qb = qp[r * WINDOW:(r + 1) * WINDOW]
            mask = first_ok if r == 0 else band_ok
            o = None
            for half in range(2):
                sink = sinks_ref[2 * m + half]
                kb = kv_scr[j * 2 + half, r * WINDOW:(r + 2) * WINDOW, :]
                vb = kv_scr[4 + j * 2 + half, r * WINDOW:(r + 2) * WINDOW, :]
                s = lax.dot_general(qb, kb, (((1,), (1,)), ((), ())), preferred_element_type=_F32)
                s = jnp.where(mask, s, NEG_INF)
                mx = jnp.maximum(jnp.max(s, axis=-1, keepdims=True), sink)
                p = jnp.exp(s - mx)
                denom = jnp.sum(p, axis=-1, keepdims=True) + jnp.exp(sink - mx)
                oh = _dot(p.astype(_BF16), vb) * (1.0 / denom)
                o = oh if o is None else o + oh
            attn_scr[r * WINDOW:(r + 1) * WINDOW, m * LANES:(m + 1) * LANES] = o.astype(_BF16)
    y_attn = _dot(attn_scr[...], wao_ref[...])

    cb = _dot(h, win_ref[:, _CB0:_CB0 + CONV_WIDTH])
    cc = _dot(h, win_ref[:, _CC0:_CC0 + CONV_WIDTH])
    cx = _dot(h, win_ref[:, _CX0:_CX0 + CONV_WIDTH])
    z = cc * cx
    z_scr[SUBLANES:SUBLANES + tm, :] = z
    z1 = z_scr[SUBLANES - 1:SUBLANES - 1 + tm, :]
    z2 = z_scr[SUBLANES - 2:SUBLANES - 2 + tm, :]
    conv = cw_ref[0:1, :] * z2 + cw_ref[1:2, :] * z1 + cw_ref[2:3, :] * z
    y_conv = _dot((cb * conv).astype(_BF16), wco_ref[...])

    def gate(n):
        c0 = _G0 + n * D_MODEL
        return _sigmoid(_dot(h, win_ref[:, c0:c0 + D_MODEL]) + bg_ref[:, n * D_MODEL:(n + 1) * D_MODEL])

    part = gate(0) * y_attn + gate(1) * y_conv
    part_ref[0] = part.astype(_BF16)
    g2_ref[0] = gate(2).astype(_BF16)

    u = _dot(h, win_ref[:, _U0:_U0 + SSM_WIDTH])
    nchunk = tm // SSM_CHUNK
    for sl in range(SSM_SLABS):
        u_scr[sl] = u[:, sl * LANES:(sl + 1) * LANES]
        for st in range(SSM_CHUNK):
            u_ref[sl, :, st * LANES:(st + 1) * LANES] = (
                u_scr[sl, pl.ds(st, nchunk, stride=SSM_CHUNK), :].astype(_BF16))

    kv_scr[:, 0:WINDOW, :] = kv_scr[:, tm:tm + WINDOW, :]
    z_scr[0:SUBLANES, :] = z_scr[tm:tm + SUBLANES, :]


def _mixer_in(x, sinks, nw, win, bg, cos_t, sa_t, sb_t, cw, wao, wco):
    b, l, _ = x.shape
    tm = TOKEN_TILE
    nchunk = tm // SSM_CHUNK
    grid = (b, l // tm)
    in_specs = [
        pl.BlockSpec(memory_space=pltpu.SMEM),
        pl.BlockSpec((1, tm, D_MODEL), lambda bi, li: (bi, li, 0)),
        _const_spec((1, D_MODEL)),
        _const_spec((D_MODEL, IN_COLS)),
        _const_spec((1, GATE_WIDTH)),
        pl.BlockSpec((tm, LANES), lambda bi, li: (li, 0)),
        pl.BlockSpec((tm, LANES), lambda bi, li: (li, 0)),
        pl.BlockSpec((tm, LANES), lambda bi, li: (li, 0)),
        _const_spec((CONV_K, CONV_WIDTH)),
        _const_spec((ATTN_WIDTH, D_MODEL)),
        _const_spec((CONV_WIDTH, D_MODEL)),
    ]
    out_specs = [
        pl.BlockSpec((1, tm, D_MODEL), lambda bi, li: (bi, li, 0)),
        pl.BlockSpec((1, tm, D_MODEL), lambda bi, li: (bi, li, 0)),
        pl.BlockSpec((SSM_SLABS, nchunk, SSM_CHUNK * LANES), lambda bi, li: (0, li, bi)),
    ]
    out_shape = [
        jax.ShapeDtypeStruct((b, l, D_MODEL), _BF16),
        jax.ShapeDtypeStruct((b, l, D_MODEL), _BF16),
        jax.ShapeDtypeStruct((SSM_SLABS, l // SSM_CHUNK, b * SSM_CHUNK * LANES), _BF16),
    ]
    scratch = [
        pltpu.VMEM((8, WINDOW + tm, LANES), _BF16),
        pltpu.VMEM((SUBLANES + tm, CONV_WIDTH), _F32),
        pltpu.VMEM((SSM_SLABS, tm, LANES), _F32),
        pltpu.VMEM((tm, ATTN_WIDTH), _BF16),
    ]
    return pl.pallas_call(
        _mixer_in_kernel, grid=grid, in_specs=in_specs, out_specs=out_specs, out_shape=out_shape,
        scratch_shapes=scratch, name="mixer_in",
        compiler_params=pltpu.CompilerParams(
            dimension_semantics=("arbitrary", "arbitrary"), vmem_limit_bytes=VMEM_LIMIT),
    )(sinks, x, nw, win, bg, cos_t, sa_t, sb_t, cw, wao, wco)


def _ssm_kernel(u_ref, mb_ref, mt_ref, mc_ref, ar_ref, ai_ref, y_ref, s_scr, *, batch):
    rows = u_ref.shape[1]
    rb = min(SSM_ROW_BLOCK, rows)

    for i in range(rows // rb):
        s_scr[i * rb:(i + 1) * rb, :] = _dot(u_ref[0, i * rb:(i + 1) * rb, :], mb_ref[0])

    ar = jnp.broadcast_to(ar_ref[0], (batch, SLAB_STATE))
    ai = jnp.broadcast_to(ai_ref[0], (batch, SLAB_STATE))

    def step(c, carry):
        xr, xi = carry
        r0 = pl.multiple_of(c * batch, batch)
        sr = s_scr[pl.ds(r0, batch), 0:SLAB_STATE]
        si = s_scr[pl.ds(r0, batch), SLAB_STATE:2 * SLAB_STATE]
        s_scr[pl.ds(r0, batch), 0:SLAB_STATE] = xr
        s_scr[pl.ds(r0, batch), SLAB_STATE:2 * SLAB_STATE] = xi
        return ar * xr - ai * xi + sr, ar * xi + ai * xr + si

    zero = jnp.zeros((batch, SLAB_STATE), _F32)
    lax.fori_loop(0, rows // batch, step, (zero, zero))

    for i in range(rows // rb):
        sl = slice(i * rb, (i + 1) * rb)
        y = _dot(u_ref[0, sl, :], mt_ref[0]) + _dot(s_scr[sl, :].astype(_BF16), mc_ref[0])
        y_ref[0, sl, :] = y.astype(_BF16)


def _ssm(u, mb, mt, mc, ar, ai, batch):
    nslab, rows, width = u.shape
    spec3 = lambda s1, s2: pl.BlockSpec((1, s1, s2), lambda q: (q, 0, 0))
    return pl.pallas_call(
        functools.partial(_ssm_kernel, batch=batch),
        grid=(nslab,),
        in_specs=[spec3(rows, width), spec3(width, 2 * SLAB_STATE), spec3(width, width),
                  spec3(2 * SLAB_STATE, width), spec3(1, SLAB_STATE), spec3(1, SLAB_STATE)],
        out_specs=spec3(rows, width),
        out_shape=jax.ShapeDtypeStruct((nslab, rows, width), _BF16),
        scratch_shapes=[pltpu.VMEM((rows, 2 * SLAB_STATE), _F32)],
        name="ssm",
        compiler_params=pltpu.CompilerParams(
            dimension_semantics=("arbitrary",), vmem_limit_bytes=VMEM_LIMIT),
    )(u, mb, mt, mc, ar, ai)


def _ssm_operators(a_re, a_im, b_re, b_im, c_re, c_im, d, log_dt):
    t = SSM_CHUNK
    g, p, hh = SSM_GROUPS, SSM_STATE, SSM_GROUP
    a_re, a_im = a_re.astype(_F32), a_im.astype(_F32)
    dt = jnp.exp(log_dt.astype(_F32))[:, None]
    n = jnp.arange(t + 1, dtype=_F32)[:, None, None]
    mag = jnp.exp(a_re * dt * n)
    ang = a_im * dt * n
    pw_re, pw_im = mag * jnp.cos(ang), mag * jnp.sin(ang)
    nr, ni = pw_re[1] - 1.0, pw_im[1]
    den = a_re * a_re + a_im * a_im
    f_re, f_im = (nr * a_re + ni * a_im) / den, (ni * a_re - nr * a_im) / den
    b_re, b_im = b_re.astype(_F32), b_im.astype(_F32)
    bb_re = f_re[..., None] * b_re - f_im[..., None] * b_im
    bb_im = f_re[..., None] * b_im + f_im[..., None] * b_re
    c_re, c_im = c_re.astype(_F32), c_im.astype(_F32)
    cp_re = c_re[None] * pw_re[:, :, None, :] - c_im[None] * pw_im[:, :, None, :]
    cp_im = c_re[None] * pw_im[:, :, None, :] + c_im[None] * pw_re[:, :, None, :]
    hi = lax.Precision.HIGHEST
    kern = (jnp.einsum("dghp,gpk->dghk", cp_re[:t], bb_re, precision=hi)
            - jnp.einsum("dghp,gpk->dghk", cp_im[:t], bb_im, precision=hi))
    kern = kern.at[0].add(jnp.eye(hh, dtype=_F32)[None] * d.astype(_F32).reshape(g, hh, 1))
    eye = jnp.eye(SLAB_GROUPS, dtype=_F32)

    lag = jnp.arange(t)[None, :] - jnp.arange(t)[:, None]
    toe = jnp.where((lag >= 0)[:, :, None, None, None], kern[jnp.clip(lag, 0, t - 1)], 0.0)
    toe = toe.reshape(t, t, SSM_SLABS, SLAB_GROUPS, hh, hh)
    mt = jnp.einsum("asqghk,Gg->qaGksgh", toe, eye).reshape(SSM_SLABS, t * LANES, t * LANES)

    rev = pw_re[t - 1 - jnp.arange(t)], pw_im[t - 1 - jnp.arange(t)]
    w_re = rev[0][..., None] * bb_re[None] - rev[1][..., None] * bb_im[None]
    w_im = rev[0][..., None] * bb_im[None] + rev[1][..., None] * bb_re[None]
    w = jnp.stack([w_re, w_im], axis=0).reshape(2, t, SSM_SLABS, SLAB_GROUPS, p, hh)
    mb = jnp.einsum("casgpk,Gg->saGkcgp", w, eye).reshape(SSM_SLABS, t * LANES, 2 * SLAB_STATE)

    cm = jnp.stack([cp_re[1:], -cp_im[1:]], axis=0).reshape(2, t, SSM_SLABS, SLAB_GROUPS, hh, p)
    mc = jnp.einsum("casghp,Gg->scGpagh", cm, eye).reshape(SSM_SLABS, 2 * SLAB_STATE, t * LANES)

    ar = pw_re[t].reshape(SSM_SLABS, 1, SLAB_STATE)
    ai = pw_im[t].reshape(SSM_SLABS, 1, SLAB_STATE)
    return mb.astype(_BF16), mt.astype(_BF16), mc.astype(_BF16), ar, ai


def _mixer_out_kernel(x_ref, part_ref, g2_ref, y_ref, wglu_ref, wso_ref, wmix_ref, nf_ref,
                      wfi_ref, wfo_ref, nlast_ref, out_ref, ys_scr, *, final):
    tm = x_ref.shape[1]
    nchunk = tm // SSM_CHUNK
    for sl in range(SSM_SLABS):
        for st in range(SSM_CHUNK):
            ys_scr[sl, pl.ds(st, nchunk, stride=SSM_CHUNK), :] = (
                y_ref[sl, :, st * LANES:(st + 1) * LANES].astype(_F32))
    y = jnp.concatenate([ys_scr[sl] for sl in range(SSM_SLABS)], axis=1)
    ys = 0.5 * y * (1.0 + jnp.tanh(math.sqrt(2.0 / math.pi) * (y + 0.044715 * (y * y * y))))
    glu = ys * _sigmoid(_dot(ys.astype(_BF16), wglu_ref[...]))
    y_ssm = _dot(glu.astype(_BF16), wso_ref[...])
    merged = part_ref[0].astype(_F32) + g2_ref[0].astype(_F32) * y_ssm
    x1 = x_ref[0] + _dot(merged.astype(_BF16), wmix_ref[...])

    h = _rmsnorm(x1, nf_ref[...]).astype(_BF16)
    gt = _dot(h, wfi_ref[:, 0:FFN_HIDDEN])
    up = _dot(h, wfi_ref[:, FFN_HIDDEN:2 * FFN_HIDDEN])
    act = (gt * _sigmoid(gt) * up).astype(_BF16)
    x2 = x1 + _dot(act, wfo_ref[...])
    if final:
        x2 = _rmsnorm(x2, nlast_ref[...])
    out_ref[0] = x2


def _mixer_out(x, part, g2, y, wglu, wso, wmix, nf, wfi, wfo, nlast, final):
    b, l, _ = x.shape
    tm = TOKEN_TILE
    nchunk = tm // SSM_CHUNK
    tok = pl.BlockSpec((1, tm, D_MODEL), lambda bi, li: (bi, li, 0))
    in_specs = [
        tok, tok, tok,
        pl.BlockSpec((SSM_SLABS, nchunk, SSM_CHUNK * LANES), lambda bi, li: (0, li, bi)),
        _const_spec((SSM_WIDTH, SSM_WIDTH)),
        _const_spec((SSM_WIDTH, D_MODEL)),
        _const_spec((D_MODEL, D_MODEL)),
        _const_spec((1, D_MODEL)),
        _const_spec((D_MODEL, 2 * FFN_HIDDEN)),
        _const_spec((FFN_HIDDEN, D_MODEL)),
        _const_spec((1, D_MODEL)),
    ]
    return pl.pallas_call(
        functools.partial(_mixer_out_kernel, final=final),
        grid=(b, l // tm), in_specs=in_specs, out_specs=tok,
        out_shape=jax.ShapeDtypeStruct((b, l, D_MODEL), _F32),
        scratch_shapes=[pltpu.VMEM((SSM_SLABS, tm, LANES), _F32)],
        name="mixer_out",
        compiler_params=pltpu.CompilerParams(
            dimension_semantics=("arbitrary", "arbitrary"), vmem_limit_bytes=VMEM_LIMIT),
    )(x, part, g2, y, wglu, wso, wmix, nf, wfi, wfo, nlast)


def _rope_tables(seq_len):
    half = ROT_DIM // 2
    pos = jnp.arange(seq_len, dtype=_F32)
    inv_freq = ROPE_THETA ** (-jnp.arange(0, ROT_DIM, 2, dtype=_F32) / ROT_DIM)
    ang = pos[:, None] * inv_freq[None, :]
    cos, sin = jnp.cos(ang), jnp.sin(ang)
    ones = jnp.ones((seq_len, HEAD_DIM - ROT_DIM), _F32)
    zeros_h = jnp.zeros((seq_len, half), _F32)
    zeros_r = jnp.zeros((seq_len, HEAD_DIM - ROT_DIM), _F32)
    cos_h = jnp.concatenate([cos, cos, ones], axis=1)
    sa_h = jnp.concatenate([-sin, zeros_h, zeros_r], axis=1)
    sb_h = jnp.concatenate([zeros_h, sin, zeros_r], axis=1)
    rep = LANES // HEAD_DIM
    return jnp.tile(cos_h, (1, rep)), jnp.tile(sa_h, (1, rep)), jnp.tile(sb_h, (1, rep))


def kernel(x, norm_mix, w_in, b_gate, attn_sinks, w_attn_o, conv_w, w_conv_o, ssm_a_re, ssm_a_im,
           ssm_b_re, ssm_b_im, ssm_c_re, ssm_c_im, ssm_d, ssm_log_dt, w_ssm_glu, w_ssm_o, w_mix_o,
           norm_ffn, w_ffn_in, w_ffn_out, norm_final):
    b, l, d_model = x.shape
    depth = w_in.shape[0]
    assert d_model == D_MODEL and l % TOKEN_TILE == 0 and b % SUBLANES == 0
    assert w_in.shape[2] == IN_COLS and w_ffn_out.shape[1] == FFN_HIDDEN
    cos_t, sa_t, sb_t = _rope_tables(l)
    bf = lambda w: w.astype(_BF16)
    nlast = norm_final.reshape(1, D_MODEL).astype(_F32)
    x = x.astype(_F32)
    for i in range(depth):
        part, g2, u = _mixer_in(
            x, attn_sinks[i].astype(_F32), norm_mix[i].reshape(1, D_MODEL).astype(_F32), bf(w_in[i]),
            b_gate[i].reshape(1, GATE_WIDTH).astype(_F32), cos_t, sa_t, sb_t, conv_w[i].astype(_F32),
            bf(w_attn_o[i]), bf(w_conv_o[i]))
        mb, mt, mc, ar, ai = _ssm_operators(
            ssm_a_re[i], ssm_a_im[i], ssm_b_re[i], ssm_b_im[i], ssm_c_re[i], ssm_c_im[i],
            ssm_d[i], ssm_log_dt[i])
        rows = (l // SSM_CHUNK) * b
        y = _ssm(u.reshape(SSM_SLABS, rows, SSM_CHUNK * LANES), mb, mt, mc, ar, ai, b)
        y = y.reshape(SSM_SLABS, l // SSM_CHUNK, b * SSM_CHUNK * LANES)
        x = _mixer_out(
            x, part, g2, y, bf(w_ssm_glu[i]), bf(w_ssm_o[i]), bf(w_mix_o[i]),
            norm_ffn[i].reshape(1, D_MODEL).astype(_F32), bf(w_ffn_in[i]), bf(w_ffn_out[i]),
            nlast, i == depth - 1)
    return x
```

```python
import functools
import math

import jax
import jax.numpy as jnp
from jax import lax
from jax.experimental import pallas as pl
from jax.experimental.pallas import tpu as pltpu

D_MODEL = 1024
N_Q_HEADS = 8
N_KV_HEADS = 2
HEAD_DIM = 64
Q_GROUP = N_Q_HEADS // N_KV_HEADS
WINDOW = 128
ROPE_THETA = 500000.0
ROT_DIM = HEAD_DIM // 4
ATTN_WIDTH = N_Q_HEADS * HEAD_DIM
KV_WIDTH = N_KV_HEADS * HEAD_DIM
NEG_INF = -1e30
CONV_WIDTH = D_MODEL // 2
CONV_K = 3
SSM_WIDTH = D_MODEL // 2
SSM_GROUP = 16
SSM_GROUPS = SSM_WIDTH // SSM_GROUP
SSM_STATE = 64
GATE_WIDTH = 3 * D_MODEL
FFN_HIDDEN = 2816
NORM_EPS = 1e-6

_Q0 = 0
_KV0 = _Q0 + ATTN_WIDTH
_CB0 = _KV0 + 2 * KV_WIDTH
_CC0 = _CB0 + CONV_WIDTH
_CX0 = _CC0 + CONV_WIDTH
_U0 = _CX0 + CONV_WIDTH
_G0 = _U0 + SSM_WIDTH
IN_COLS = _G0 + GATE_WIDTH

LANES = 128
SUBLANES = 8
SSM_CHUNK = 8
SSM_SLABS = SSM_WIDTH // LANES
SLAB_GROUPS = LANES // SSM_GROUP
SLAB_STATE = SLAB_GROUPS * SSM_STATE
TOKEN_TILE = 256
SSM_CHUNK_BLOCK = 64
VMEM_LIMIT = 56 * 1024 * 1024

_BF16 = jnp.bfloat16
_F32 = jnp.float32


def _dot(a, b):
    return jnp.dot(a, b, preferred_element_type=_F32)


def _sigmoid(x):
    return 0.5 * jnp.tanh(0.5 * x) + 0.5


def _rmsnorm(x, w):
    ms = jnp.mean(x * x, axis=-1, keepdims=True)
    return x * lax.rsqrt(ms + NORM_EPS) * w


def _const_spec(shape):
    return pl.BlockSpec(shape, lambda *_: (0,) * len(shape), pipeline_mode=pl.Buffered(1))


def _ssm_io_spec(nchunk):
    return pl.BlockSpec((SSM_SLABS, nchunk, None, SSM_CHUNK, LANES), lambda bi, li: (0, li, bi, 0, 0))


def _mixer_in_kernel(sinks_ref, x_ref, nw_ref, win_ref, bg_ref, cos_ref, sa_ref, sb_ref, cw_ref,
                     wao_ref, wco_ref, part_ref, g2_ref, u_ref,
                     kv_scr, z_scr, attn_scr):
    tm = x_ref.shape[1]
    li = pl.program_id(1)

    @pl.when(li == 0)
    def _():
        kv_scr[:, 0:WINDOW, :] = jnp.zeros((8, WINDOW, LANES), _BF16)
        z_scr[0:SUBLANES, :] = jnp.zeros((SUBLANES, CONV_WIDTH), _F32)

    x = x_ref[0]
    h = _rmsnorm(x, nw_ref[...]).astype(_BF16)

    cos = cos_ref[...]
    sa = sa_ref[...]
    sb = sb_ref[...]

    def rope(t):
        return (t * cos + pltpu.roll(t, LANES - ROT_DIM // 2, axis=1) * sa
                + pltpu.roll(t, ROT_DIM // 2, axis=1) * sb)

    kv = _dot(h, win_ref[:, _KV0:_KV0 + 2 * KV_WIDTH])
    lane = lax.broadcasted_iota(jnp.int32, (tm, LANES), 1)
    lo = lane < HEAD_DIM
    for kind in range(2):
        t = kv[:, kind * LANES:(kind + 1) * LANES]
        if kind == 0:
            t = rope(t)
        t_sw = pltpu.roll(t, HEAD_DIM, axis=1)
        zero = jnp.zeros_like(t)
        variants = (jnp.where(lo, t, zero), jnp.where(lo, zero, t_sw),
                    jnp.where(lo, t_sw, zero), jnp.where(lo, zero, t))
        for n, val in enumerate(variants):
            kv_scr[kind * 4 + n, WINDOW:WINDOW + tm, :] = val.astype(_BF16)

    q = _dot(h, win_ref[:, _Q0:_Q0 + ATTN_WIDTH])
    row = lax.broadcasted_iota(jnp.int32, (WINDOW, 2 * WINDOW), 0)
    col = lax.broadcasted_iota(jnp.int32, (WINDOW, 2 * WINDOW), 1)
    band_ok = (col > row) & (col <= row + WINDOW)
    first_ok = band_ok & (col >= jnp.where(li > 0, 0, WINDOW))
    for m in range(N_Q_HEADS // 2):
        qp = (rope(q[:, m * LANES:(m + 1) * LANES]) * (HEAD_DIM ** -0.5)).astype(_BF16)
        j = (2 * m) // Q_GROUP
        for r in range(tm // WINDOW):
            qb = qp[r * WINDOW:(r + 1) * WINDOW]
            mask = first_ok if r == 0 else band_ok
            o = None
            for half in range(2):
                sink = sinks_ref[2 * m + half]
                kb = kv_scr[j * 2 + half, r * WINDOW:(r + 2) * WINDOW, :]
                vb = kv_scr[4 + j * 2 + half, r * WINDOW:(r + 2) * WINDOW, :]
                s = lax.dot_general(qb, kb, (((1,), (1,)), ((), ())), preferred_element_type=_F32)
                s = jnp.where(mask, s, NEG_INF)
                mx = jnp.maximum(jnp.max(s, axis=-1, keepdims=True), sink)
                p = jnp.exp(s - mx)
                denom = jnp.sum(p, axis=-1, keepdims=True) + jnp.exp(sink - mx)
                oh = _dot(p.astype(_BF16), vb) * (1.0 / denom)
                o = oh if o is None else o + oh
            attn_scr[r * WINDOW:(r + 1) * WINDOW, m * LANES:(m + 1) * LANES] = o.astype(_BF16)
    y_attn = _dot(attn_scr[...], wao_ref[...])

    cb = _dot(h, win_ref[:, _CB0:_CB0 + CONV_WIDTH])
    cc = _dot(h, win_ref[:, _CC0:_CC0 + CONV_WIDTH])
    cx = _dot(h, win_ref[:, _CX0:_CX0 + CONV_WIDTH])
    z = cc * cx
    z_scr[SUBLANES:SUBLANES + tm, :] = z
    z1 = z_scr[SUBLANES - 1:SUBLANES - 1 + tm, :]
    z2 = z_scr[SUBLANES - 2:SUBLANES - 2 + tm, :]
    conv = cw_ref[0:1, :] * z2 + cw_ref[1:2, :] * z1 + cw_ref[2:3, :] * z
    y_conv = _dot((cb * conv).astype(_BF16), wco_ref[...])

    def gate(n):
        c0 = _G0 + n * D_MODEL
        return _sigmoid(_dot(h, win_ref[:, c0:c0 + D_MODEL]) + bg_ref[:, n * D_MODEL:(n + 1) * D_MODEL])

    part = gate(0) * y_attn + gate(1) * y_conv
    part_ref[0] = part.astype(_BF16)
    g2_ref[0] = gate(2).astype(_BF16)

    u = _dot(h, win_ref[:, _U0:_U0 + SSM_WIDTH])
    nchunk = tm // SSM_CHUNK
    for sl in range(SSM_SLABS):
        u_ref[sl] = u[:, sl * LANES:(sl + 1) * LANES].reshape(nchunk, SSM_CHUNK, LANES)

    kv_scr[:, 0:WINDOW, :] = kv_scr[:, tm:tm + WINDOW, :]
    z_scr[0:SUBLANES, :] = z_scr[tm:tm + SUBLANES, :]


def _mixer_in(x, sinks, nw, win, bg, cos_t, sa_t, sb_t, cw, wao, wco):
    b, l, _ = x.shape
    tm = TOKEN_TILE
    nchunk = tm // SSM_CHUNK
    grid = (b, l // tm)
    in_specs = [
        pl.BlockSpec(memory_space=pltpu.SMEM),
        pl.BlockSpec((1, tm, D_MODEL), lambda bi, li: (bi, li, 0)),
        _const_spec((1, D_MODEL)),
        _const_spec((D_MODEL, IN_COLS)),
        _const_spec((1, GATE_WIDTH)),
        pl.BlockSpec((tm, LANES), lambda bi, li: (li, 0)),
        pl.BlockSpec((tm, LANES), lambda bi, li: (li, 0)),
        pl.BlockSpec((tm, LANES), lambda bi, li: (li, 0)),
        _const_spec((CONV_K, CONV_WIDTH)),
        _const_spec((ATTN_WIDTH, D_MODEL)),
        _const_spec((CONV_WIDTH, D_MODEL)),
    ]
    out_specs = [
        pl.BlockSpec((1, tm, D_MODEL), lambda bi, li: (bi, li, 0)),
        pl.BlockSpec((1, tm, D_MODEL), lambda bi, li: (bi, li, 0)),
        _ssm_io_spec(nchunk),
    ]
    out_shape = [
        jax.ShapeDtypeStruct((b, l, D_MODEL), _BF16),
        jax.ShapeDtypeStruct((b, l, D_MODEL), _BF16),
        jax.ShapeDtypeStruct((SSM_SLABS, l // SSM_CHUNK, b, SSM_CHUNK, LANES), _F32),
    ]
    scratch = [
        pltpu.VMEM((8, WINDOW + tm, LANES), _BF16),
        pltpu.VMEM((SUBLANES + tm, CONV_WIDTH), _F32),
        pltpu.VMEM((tm, ATTN_WIDTH), _BF16),
    ]
    return pl.pallas_call(
        _mixer_in_kernel, grid=grid, in_specs=in_specs, out_specs=out_specs, out_shape=out_shape,
        scratch_shapes=scratch, name="mixer_in",
        compiler_params=pltpu.CompilerParams(
            dimension_semantics=("arbitrary", "arbitrary"), vmem_limit_bytes=VMEM_LIMIT),
    )(sinks, x, nw, win, bg, cos_t, sa_t, sb_t, cw, wao, wco)


def _ssm_kernel(u_ref, wc_ref, cct_ref, kc_ref, ar_ref, ai_ref, y_ref,
                mb_scr, mt_scr, mct_scr, ucat_scr, s_scr, xr_scr, xi_scr, *, batch):
    t = SSM_CHUNK
    rows = u_ref.shape[1] // t

    @pl.when(pl.program_id(1) == 0)
    def _():
        chan_bits, state_bits = SSM_GROUP.bit_length() - 1, SSM_STATE.bit_length() - 1
        r_grp = lax.broadcasted_iota(jnp.int32, (LANES, 2 * SLAB_STATE), 0) >> chan_bits
        c_grp = (lax.broadcasted_iota(jnp.int32, (LANES, 2 * SLAB_STATE), 1) & (SLAB_STATE - 1)) >> state_bits
        same_state = r_grp == c_grp
        r_grp = lax.broadcasted_iota(jnp.int32, (LANES, LANES), 0) >> chan_bits
        c_grp = lax.broadcasted_iota(jnp.int32, (LANES, LANES), 1) >> chan_bits
        same_chan = r_grp == c_grp
        zero_blk = jnp.zeros((LANES, LANES), _BF16)
        blks = []
        for st in range(t):
            rs = slice(st * LANES, (st + 1) * LANES)
            mb_scr[rs, :] = jnp.where(same_state, jnp.tile(wc_ref[0, st], (SLAB_GROUPS, 1)), 0.0).astype(_BF16)
            mct_scr[rs, :] = jnp.where(same_state, jnp.tile(cct_ref[0, st], (SLAB_GROUPS, 1)), 0.0).astype(_BF16)
            blks.append(jnp.where(same_chan, jnp.tile(kc_ref[0, st], (SLAB_GROUPS, 1)), 0.0).astype(_BF16))
        for s_in in range(t):
            for s_out in range(t):
                mt_scr[s_in * LANES:(s_in + 1) * LANES, s_out * LANES:(s_out + 1) * LANES] = (
                    blks[s_out - s_in] if s_out >= s_in else zero_blk)
        xr_scr[...] = jnp.zeros_like(xr_scr)
        xi_scr[...] = jnp.zeros_like(xi_scr)

    for st in range(t):
        ucat_scr[:, st * LANES:(st + 1) * LANES] = u_ref[0, pl.ds(st, rows, stride=t), :].astype(_BF16)

    s_scr[...] = _dot(ucat_scr[...], mb_scr[...])

    ar = jnp.broadcast_to(ar_ref[0], (batch, SLAB_STATE))
    ai = jnp.broadcast_to(ai_ref[0], (batch, SLAB_STATE))

    def step(c, carry):
        xr, xi = carry
        r0 = pl.multiple_of(c * batch, batch)
        sr = s_scr[pl.ds(r0, batch), 0:SLAB_STATE]
        si = s_scr[pl.ds(r0, batch), SLAB_STATE:2 * SLAB_STATE]
        s_scr[pl.ds(r0, batch), 0:SLAB_STATE] = xr
        s_scr[pl.ds(r0, batch), SLAB_STATE:2 * SLAB_STATE] = xi
        return ar * xr - ai * xi + sr, ar * xi + ai * xr + si

    xr, xi = lax.fori_loop(0, rows // batch, step, (xr_scr[...], xi_scr[...]))
    xr_scr[...] = xr
    xi_scr[...] = xi

    y = _dot(ucat_scr[...], mt_scr[...]) + lax.dot_general(
        s_scr[...].astype(_BF16), mct_scr[...], (((1,), (1,)), ((), ())), preferred_element_type=_F32)
    for st in range(t):
        y_ref[0, pl.ds(st, rows, stride=t), :] = y[:, st * LANES:(st + 1) * LANES]


def _ssm(u, wc, cct, kc, ar, ai, batch):
    nslab, tok_rows, _ = u.shape
    blk = SSM_CHUNK_BLOCK * batch * SSM_CHUNK
    rows = SSM_CHUNK_BLOCK * batch
    width = SSM_CHUNK * LANES
    io = pl.BlockSpec((1, blk, LANES), lambda q, ci: (q, ci, 0))
    par = lambda *shape: pl.BlockSpec((1,) + shape, lambda q, ci: (q,) + (0,) * len(shape))
    return pl.pallas_call(
        functools.partial(_ssm_kernel, batch=batch),
        grid=(nslab, tok_rows // blk),
        in_specs=[io, par(SSM_CHUNK, SSM_GROUP, 2 * SLAB_STATE), par(SSM_CHUNK, SSM_GROUP, 2 * SLAB_STATE),
                  par(SSM_CHUNK, SSM_GROUP, LANES), par(1, SLAB_STATE), par(1, SLAB_STATE)],
        out_specs=io,
        out_shape=jax.ShapeDtypeStruct(u.shape, _F32),
        scratch_shapes=[
            pltpu.VMEM((width, 2 * SLAB_STATE), _BF16),
            pltpu.VMEM((width, width), _BF16),
            pltpu.VMEM((width, 2 * SLAB_STATE), _BF16),
            pltpu.VMEM((rows, width), _BF16),
            pltpu.VMEM((rows, 2 * SLAB_STATE), _F32),
            pltpu.VMEM((batch, SLAB_STATE), _F32),
            pltpu.VMEM((batch, SLAB_STATE), _F32),
        ],
        name="ssm",
        compiler_params=pltpu.CompilerParams(
            dimension_semantics=("arbitrary", "arbitrary"), vmem_limit_bytes=VMEM_LIMIT),
    )(u, wc, cct, kc, ar, ai)


def _ssm_operators(a_re, a_im, b_re, b_im, c_re, c_im, d, log_dt):
    t = SSM_CHUNK
    g, p, hh = SSM_GROUPS, SSM_STATE, SSM_GROUP
    a_re, a_im = a_re.astype(_F32), a_im.astype(_F32)
    dt = jnp.exp(log_dt.astype(_F32))[:, None]
    n = jnp.arange(t + 1, dtype=_F32)[:, None, None]
    mag = jnp.exp(a_re * dt * n)
    ang = a_im * dt * n
    pw_re, pw_im = mag * jnp.cos(ang), mag * jnp.sin(ang)
    nr, ni = pw_re[1] - 1.0, pw_im[1]
    den = a_re * a_re + a_im * a_im
    f_re, f_im = (nr * a_re + ni * a_im) / den, (ni * a_re - nr * a_im) / den
    b_re, b_im = b_re.astype(_F32), b_im.astype(_F32)
    bb_re = f_re[..., None] * b_re - f_im[..., None] * b_im
    bb_im = f_re[..., None] * b_im + f_im[..., None] * b_re
    c_re, c_im = c_re.astype(_F32), c_im.astype(_F32)
    cp_re = c_re[None] * pw_re[:, :, None, :] - c_im[None] * pw_im[:, :, None, :]
    cp_im = c_re[None] * pw_im[:, :, None, :] + c_im[None] * pw_re[:, :, None, :]
    hi = lax.Precision.HIGHEST
    kern = (jnp.einsum("dghp,gpk->dghk", cp_re[:t], bb_re, precision=hi)
            - jnp.einsum("dghp,gpk->dghk", cp_im[:t], bb_im, precision=hi))
    kern = kern.at[0].add(jnp.eye(hh, dtype=_F32)[None] * d.astype(_F32).reshape(g, hh, 1))
    kc = kern.reshape(t, SSM_SLABS, SLAB_GROUPS, hh, hh).transpose(1, 0, 4, 2, 3).reshape(SSM_SLABS, t, hh, LANES)

    rev = t - 1 - jnp.arange(t)
    rev_re, rev_im = pw_re[rev], pw_im[rev]
    w_re = rev_re[..., None] * bb_re[None] - rev_im[..., None] * bb_im[None]
    w_im = rev_re[..., None] * bb_im[None] + rev_im[..., None] * bb_re[None]
    w = jnp.stack([w_re, w_im], axis=0).reshape(2, t, SSM_SLABS, SLAB_GROUPS, p, hh)
    wc = w.transpose(2, 1, 5, 0, 3, 4).reshape(SSM_SLABS, t, hh, 2 * SLAB_STATE)

    cm = jnp.stack([cp_re[1:], -cp_im[1:]], axis=0).reshape(2, t, SSM_SLABS, SLAB_GROUPS, hh, p)
    cct = cm.transpose(2, 1, 4, 0, 3, 5).reshape(SSM_SLABS, t, hh, 2 * SLAB_STATE)

    ar = pw_re[t].reshape(SSM_SLABS, 1, SLAB_STATE)
    ai = pw_im[t].reshape(SSM_SLABS, 1, SLAB_STATE)
    return wc, cct, kc, ar, ai


def _mixer_out_kernel(x_ref, part_ref, g2_ref, y_ref, wglu_ref, wso_ref, wmix_ref, nf_ref,
                      wfi_ref, wfo_ref, nlast_ref, out_ref, *, final):
    tm = x_ref.shape[1]
    y = jnp.concatenate([y_ref[sl].reshape(tm, LANES) for sl in range(SSM_SLABS)], axis=1)
    ys = 0.5 * y * (1.0 + jnp.tanh(math.sqrt(2.0 / math.pi) * (y + 0.044715 * (y * y * y))))
    glu = ys * _sigmoid(_dot(ys.astype(_BF16), wglu_ref[...]))
    y_ssm = _dot(glu.astype(_BF16), wso_ref[...])
    merged = part_ref[0].astype(_F32) + g2_ref[0].astype(_F32) * y_ssm
    x1 = x_ref[0] + _dot(merged.astype(_BF16), wmix_ref[...])

    h = _rmsnorm(x1, nf_ref[...]).astype(_BF16)
    gt = _dot(h, wfi_ref[:, 0:FFN_HIDDEN])
    up = _dot(h, wfi_ref[:, FFN_HIDDEN:2 * FFN_HIDDEN])
    act = (gt * _sigmoid(gt) * up).astype(_BF16)
    x2 = x1 + _dot(act, wfo_ref[...])
    if final:
        x2 = _rmsnorm(x2, nlast_ref[...])
    out_ref[0] = x2


def _mixer_out(x, part, g2, y, wglu, wso, wmix, nf, wfi, wfo, nlast, final):
    b, l, _ = x.shape
    tm = TOKEN_TILE
    nchunk = tm // SSM_CHUNK
    tok = pl.BlockSpec((1, tm, D_MODEL), lambda bi, li: (bi, li, 0))
    in_specs = [
        tok, tok, tok,
        _ssm_io_spec(nchunk),
        _const_spec((SSM_WIDTH, SSM_WIDTH)),
        _const_spec((SSM_WIDTH, D_MODEL)),
        _const_spec((D_MODEL, D_MODEL)),
        _const_spec((1, D_MODEL)),
        _const_spec((D_MODEL, 2 * FFN_HIDDEN)),
        _const_spec((FFN_HIDDEN, D_MODEL)),
        _const_spec((1, D_MODEL)),
    ]
    return pl.pallas_call(
        functools.partial(_mixer_out_kernel, final=final),
        grid=(b, l // tm), in_specs=in_specs, out_specs=tok,
        out_shape=jax.ShapeDtypeStruct((b, l, D_MODEL), _F32),
        name="mixer_out",
        compiler_params=pltpu.CompilerParams(
            dimension_semantics=("arbitrary", "arbitrary"), vmem_limit_bytes=VMEM_LIMIT),
    )(x, part, g2, y, wglu, wso, wmix, nf, wfi, wfo, nlast)


def _rope_tables(seq_len):
    half = ROT_DIM // 2
    pos = jnp.arange(seq_len, dtype=_F32)
    inv_freq = ROPE_THETA ** (-jnp.arange(0, ROT_DIM, 2, dtype=_F32) / ROT_DIM)
    ang = pos[:, None] * inv_freq[None, :]
    cos, sin = jnp.cos(ang), jnp.sin(ang)
    ones = jnp.ones((seq_len, HEAD_DIM - ROT_DIM), _F32)
    zeros_h = jnp.zeros((seq_len, half), _F32)
    zeros_r = jnp.zeros((seq_len, HEAD_DIM - ROT_DIM), _F32)
    cos_h = jnp.concatenate([cos, cos, ones], axis=1)
    sa_h = jnp.concatenate([-sin, zeros_h, zeros_r], axis=1)
    sb_h = jnp.concatenate([zeros_h, sin, zeros_r], axis=1)
    rep = LANES // HEAD_DIM
    return jnp.tile(cos_h, (1, rep)), jnp.tile(sa_h, (1, rep)), jnp.tile(sb_h, (1, rep))


def kernel(x, norm_mix, w_in, b_gate, attn_sinks, w_attn_o, conv_w, w_conv_o, ssm_a_re, ssm_a_im,
           ssm_b_re, ssm_b_im, ssm_c_re, ssm_c_im, ssm_d, ssm_log_dt, w_ssm_glu, w_ssm_o, w_mix_o,
           norm_ffn, w_ffn_in, w_ffn_out, norm_final):
    b, l, d_model = x.shape
    depth = w_in.shape[0]
    assert d_model == D_MODEL and l % TOKEN_TILE == 0 and b % SUBLANES == 0
    assert l % (SSM_CHUNK * SSM_CHUNK_BLOCK) == 0
    assert w_in.shape[2] == IN_COLS and w_ffn_out.shape[1] == FFN_HIDDEN
    cos_t, sa_t, sb_t = _rope_tables(l)
    bf = lambda w: w.astype(_BF16)
    nlast = norm_final.reshape(1, D_MODEL).astype(_F32)
    x = x.astype(_F32)
    for i in range(depth):
        part, g2, u = _mixer_in(
            x, attn_sinks[i].astype(_F32), norm_mix[i].reshape(1, D_MODEL).astype(_F32), bf(w_in[i]),
            b_gate[i].reshape(1, GATE_WIDTH).astype(_F32), cos_t, sa_t, sb_t, conv_w[i].astype(_F32),
            bf(w_attn_o[i]), bf(w_conv_o[i]))
        wc, cct, kc, ar, ai = _ssm_operators(
            ssm_a_re[i], ssm_a_im[i], ssm_b_re[i], ssm_b_im[i], ssm_c_re[i], ssm_c_im[i],
            ssm_d[i], ssm_log_dt[i])
        y = _ssm(u.reshape(SSM_SLABS, l * b, LANES), wc, cct, kc, ar, ai, b).reshape(u.shape)
        x = _mixer_out(
            x, part, g2, y, bf(w_ssm_glu[i]), bf(w_ssm_o[i]), bf(w_mix_o[i]),
            norm_ffn[i].reshape(1, D_MODEL).astype(_F32), bf(w_ffn_in[i]), bf(w_ffn_out[i]),
            nlast, i == depth - 1)
    return x
```

```python
import functools
import math

import jax
import jax.numpy as jnp
from jax import lax
from jax.experimental import pallas as pl
from jax.experimental.pallas import tpu as pltpu

D_MODEL = 1024
N_Q_HEADS = 8
N_KV_HEADS = 2
HEAD_DIM = 64
Q_GROUP = N_Q_HEADS // N_KV_HEADS
WINDOW = 128
ROPE_THETA = 500000.0
ROT_DIM = HEAD_DIM // 4
ATTN_WIDTH = N_Q_HEADS * HEAD_DIM
KV_WIDTH = N_KV_HEADS * HEAD_DIM
NEG_INF = -1e30
CONV_WIDTH = D_MODEL // 2
CONV_K = 3
SSM_WIDTH = D_MODEL // 2
SSM_GROUP = 16
SSM_GROUPS = SSM_WIDTH // SSM_GROUP
SSM_STATE = 64
GATE_WIDTH = 3 * D_MODEL
FFN_HIDDEN = 2816
NORM_EPS = 1e-6

_Q0 = 0
_KV0 = _Q0 + ATTN_WIDTH
_CB0 = _KV0 + 2 * KV_WIDTH
_CC0 = _CB0 + CONV_WIDTH
_CX0 = _CC0 + CONV_WIDTH
_U0 = _CX0 + CONV_WIDTH
_G0 = _U0 + SSM_WIDTH
IN_COLS = _G0 + GATE_WIDTH

LANES = 128
SUBLANES = 8
SSM_CHUNK = 8
SSM_SLABS = SSM_WIDTH // LANES
SLAB_GROUPS = LANES // SSM_GROUP
SLAB_STATE = SLAB_GROUPS * SSM_STATE
TOKEN_TILE = 512
SSM_CHUNK_BLOCK = 64
VMEM_LIMIT = 56 * 1024 * 1024

_BF16 = jnp.bfloat16
_F32 = jnp.float32


def _dot(a, b):
    return jnp.dot(a, b, preferred_element_type=_F32)


def _sigmoid(x):
    return 0.5 * jnp.tanh(0.5 * x) + 0.5


def _rmsnorm(x, w):
    ms = jnp.mean(x * x, axis=-1, keepdims=True)
    return x * lax.rsqrt(ms + NORM_EPS) * w


def _const_spec(shape):
    return pl.BlockSpec(shape, lambda *_: (0,) * len(shape), pipeline_mode=pl.Buffered(1))


def _layer_spec(shape, layer):
    return pl.BlockSpec((None,) + shape, lambda *_: (layer,) + (0,) * len(shape), pipeline_mode=pl.Buffered(1))


def _ssm_io_spec(nchunk):
    return pl.BlockSpec((SSM_SLABS, nchunk, None, SSM_CHUNK, LANES), lambda bi, li: (0, li, bi, 0, 0))


def _mixer_in_kernel(sinks_ref, x_ref, nw_ref, win_ref, bg_ref, cos_ref, sa_ref, sb_ref, cw_ref,
                     wao_ref, wco_ref, part_ref, g2_ref, u_ref,
                     kv_scr, z_scr, attn_scr, yc_scr, acc_scr, g0_scr):
    tm = x_ref.shape[1]
    li = pl.program_id(1)

    @pl.when(li == 0)
    def _():
        kv_scr[:, 0:WINDOW, :] = jnp.zeros((8, WINDOW, LANES), _BF16)
        z_scr[0:SUBLANES, :] = jnp.zeros((SUBLANES, CONV_WIDTH), _F32)

    x = x_ref[0]
    h = _rmsnorm(x, nw_ref[...]).astype(_BF16)

    cos = cos_ref[...]
    sa = sa_ref[...]
    sb = sb_ref[...]

    def rope(t):
        return (t * cos + pltpu.roll(t, LANES - ROT_DIM // 2, axis=1) * sa
                + pltpu.roll(t, ROT_DIM // 2, axis=1) * sb)

    kv = _dot(h, win_ref[:, _KV0:_KV0 + 2 * KV_WIDTH])
    lane = lax.broadcasted_iota(jnp.int32, (tm, LANES), 1)
    lo = lane < HEAD_DIM
    for kind in range(2):
        t = kv[:, kind * LANES:(kind + 1) * LANES]
        if kind == 0:
            t = rope(t)
        t_sw = pltpu.roll(t, HEAD_DIM, axis=1)
        zero = jnp.zeros_like(t)
        variants = (jnp.where(lo, t, zero), jnp.where(lo, zero, t_sw),
                    jnp.where(lo, t_sw, zero), jnp.where(lo, zero, t))
        for n, val in enumerate(variants):
            kv_scr[kind * 4 + n, WINDOW:WINDOW + tm, :] = val.astype(_BF16)

    q = _dot(h, win_ref[:, _Q0:_Q0 + ATTN_WIDTH])
    row = lax.broadcasted_iota(jnp.int32, (WINDOW, 2 * WINDOW), 0)
    col = lax.broadcasted_iota(jnp.int32, (WINDOW, 2 * WINDOW), 1)
    band_ok = (col > row) & (col <= row + WINDOW)
    first_ok = band_ok & (col >= jnp.where(li > 0, 0, WINDOW))

    def gate(n):
        c0 = _G0 + n * D_MODEL
        return _sigmoid(_dot(h, win_ref[:, c0:c0 + D_MODEL]) + bg_ref[:, n * D_MODEL:(n + 1) * D_MODEL])

    def conv_in():
        cb = _dot(h, win_ref[:, _CB0:_CB0 + CONV_WIDTH])
        cc = _dot(h, win_ref[:, _CC0:_CC0 + CONV_WIDTH])
        cx = _dot(h, win_ref[:, _CX0:_CX0 + CONV_WIDTH])
        z = cc * cx
        z_scr[SUBLANES:SUBLANES + tm, :] = z
        z1 = z_scr[SUBLANES - 1:SUBLANES - 1 + tm, :]
        z2 = z_scr[SUBLANES - 2:SUBLANES - 2 + tm, :]
        conv = cw_ref[0:1, :] * z2 + cw_ref[1:2, :] * z1 + cw_ref[2:3, :] * z
        yc_scr[...] = (cb * conv).astype(_BF16)

    def conv_out():
        acc_scr[...] = gate(1) * _dot(yc_scr[...], wco_ref[...])

    def attn_gate_and_ssm_in():
        g0_scr[...] = gate(0)
        u = _dot(h, win_ref[:, _U0:_U0 + SSM_WIDTH])
        nchunk = tm // SSM_CHUNK
        for sl in range(SSM_SLABS):
            u_ref[sl] = u[:, sl * LANES:(sl + 1) * LANES].reshape(nchunk, SSM_CHUNK, LANES)

    def ssm_gate():
        g2_ref[0] = gate(2).astype(_BF16)

    side_work = (conv_in, conv_out, attn_gate_and_ssm_in, ssm_gate)

    for m in range(N_Q_HEADS // 2):
        side_work[m]()
        qp = (rope(q[:, m * LANES:(m + 1) * LANES]) * (HEAD_DIM ** -0.5)).astype(_BF16)
        j = (2 * m) // Q_GROUP
        for r in range(tm // WINDOW):
            qb = qp[r * WINDOW:(r + 1) * WINDOW]
            mask = first_ok if r == 0 else band_ok
            o = None
            for half in range(2):
                sink = sinks_ref[2 * m + half]
                kb = kv_scr[j * 2 + half, r * WINDOW:(r + 2) * WINDOW, :]
                vb = kv_scr[4 + j * 2 + half, r * WINDOW:(r + 2) * WINDOW, :]
                s = lax.dot_general(qb, kb, (((1,), (1,)), ((), ())), preferred_element_type=_F32)
                s = jnp.where(mask, s, NEG_INF)
                mx = jnp.maximum(jnp.max(s, axis=-1, keepdims=True), sink)
                p = jnp.exp(s - mx)
                denom = jnp.sum(p, axis=-1, keepdims=True) + jnp.exp(sink - mx)
                oh = _dot(p.astype(_BF16), vb) * (1.0 / denom)
                o = oh if o is None else o + oh
            attn_scr[r * WINDOW:(r + 1) * WINDOW, m * LANES:(m + 1) * LANES] = o.astype(_BF16)

    y_attn = _dot(attn_scr[...], wao_ref[...])
    part_ref[0] = (g0_scr[...] * y_attn + acc_scr[...]).astype(_BF16)

    kv_scr[:, 0:WINDOW, :] = kv_scr[:, tm:tm + WINDOW, :]
    z_scr[0:SUBLANES, :] = z_scr[tm:tm + SUBLANES, :]


def _mixer_in(x, sinks, nw, win, bg, cos_t, sa_t, sb_t, cw, wao, wco, layer):
    b, l, _ = x.shape
    tm = TOKEN_TILE
    nchunk = tm // SSM_CHUNK
    grid = (b, l // tm)
    in_specs = [
        pl.BlockSpec(memory_space=pltpu.SMEM),
        pl.BlockSpec((1, tm, D_MODEL), lambda bi, li: (bi, li, 0)),
        _layer_spec((1, D_MODEL), layer),
        _layer_spec((D_MODEL, IN_COLS), layer),
        _layer_spec((1, GATE_WIDTH), layer),
        pl.BlockSpec((tm, LANES), lambda bi, li: (li, 0)),
        pl.BlockSpec((tm, LANES), lambda bi, li: (li, 0)),
        pl.BlockSpec((tm, LANES), lambda bi, li: (li, 0)),
        _layer_spec((CONV_K, CONV_WIDTH), layer),
        _layer_spec((ATTN_WIDTH, D_MODEL), layer),
        _layer_spec((CONV_WIDTH, D_MODEL), layer),
    ]
    out_specs = [
        pl.BlockSpec((1, tm, D_MODEL), lambda bi, li: (bi, li, 0)),
        pl.BlockSpec((1, tm, D_MODEL), lambda bi, li: (bi, li, 0)),
        _ssm_io_spec(nchunk),
    ]
    out_shape = [
        jax.ShapeDtypeStruct((b, l, D_MODEL), _BF16),
        jax.ShapeDtypeStruct((b, l, D_MODEL), _BF16),
        jax.ShapeDtypeStruct((SSM_SLABS, l // SSM_CHUNK, b, SSM_CHUNK, LANES), _F32),
    ]
    scratch = [
        pltpu.VMEM((8, WINDOW + tm, LANES), _BF16),
        pltpu.VMEM((SUBLANES + tm, CONV_WIDTH), _F32),
        pltpu.VMEM((tm, ATTN_WIDTH), _BF16),
        pltpu.VMEM((tm, CONV_WIDTH), _BF16),
        pltpu.VMEM((tm, D_MODEL), _F32),
        pltpu.VMEM((tm, D_MODEL), _F32),
    ]
    return pl.pallas_call(
        _mixer_in_kernel, grid=grid, in_specs=in_specs, out_specs=out_specs, out_shape=out_shape,
        scratch_shapes=scratch, name="mixer_in",
        compiler_params=pltpu.CompilerParams(
            dimension_semantics=("arbitrary", "arbitrary"), vmem_limit_bytes=VMEM_LIMIT),
    )(sinks, x, nw, win, bg, cos_t, sa_t, sb_t, cw, wao, wco)


def _ssm_kernel(u_ref, wc_ref, cct_ref, kc_ref, ar_ref, ai_ref, y_ref,
                mb_scr, mt_scr, mct_scr, ucat_scr, s_scr, xr_scr, xi_scr, *, batch):
    t = SSM_CHUNK
    rows = u_ref.shape[1] // t

    @pl.when(pl.program_id(1) == 0)
    def _():
        chan_bits, state_bits = SSM_GROUP.bit_length() - 1, SSM_STATE.bit_length() - 1
        r_grp = lax.broadcasted_iota(jnp.int32, (LANES, 2 * SLAB_STATE), 0) >> chan_bits
        c_grp = (lax.broadcasted_iota(jnp.int32, (LANES, 2 * SLAB_STATE), 1) & (SLAB_STATE - 1)) >> state_bits
        same_state = r_grp == c_grp

        def expand(tbl):
            return jnp.where(same_state, jnp.tile(tbl, (SLAB_GROUPS, 1)), 0.0)

        r_grp = lax.broadcasted_iota(jnp.int32, (LANES, LANES), 0) >> chan_bits
        c_grp = lax.broadcasted_iota(jnp.int32, (LANES, LANES), 1) >> chan_bits
        same_chan = r_grp == c_grp
        zero_blk = jnp.zeros((LANES, LANES), _BF16)
        blks = []
        for st in range(t):
            rs = slice(st * LANES, (st + 1) * LANES)
            mb_scr[rs, :] = expand(wc_ref[0, st]).astype(_BF16)
            mct_scr[rs, :] = expand(cct_ref[0, st]).astype(_BF16)
            blks.append(jnp.where(same_chan, jnp.tile(kc_ref[0, st], (SLAB_GROUPS, 1)), 0.0).astype(_BF16))
        for s_in in range(t):
            for s_out in range(t):
                mt_scr[s_in * LANES:(s_in + 1) * LANES, s_out * LANES:(s_out + 1) * LANES] = (
                    blks[s_out - s_in] if s_out >= s_in else zero_blk)
        xr_scr[...] = jnp.zeros_like(xr_scr)
        xi_scr[...] = jnp.zeros_like(xi_scr)

    for st in range(t):
        ucat_scr[:, st * LANES:(st + 1) * LANES] = u_ref[0, pl.ds(st, rows, stride=t), :].astype(_BF16)

    s_scr[...] = _dot(ucat_scr[...], mb_scr[...])

    ar = jnp.broadcast_to(ar_ref[0], (batch, SLAB_STATE))
    ai = jnp.broadcast_to(ai_ref[0], (batch, SLAB_STATE))

    def step(c, carry):
        xr, xi = carry
        r0 = pl.multiple_of(c * batch, batch)
        sr = s_scr[pl.ds(r0, batch), 0:SLAB_STATE]
        si = s_scr[pl.ds(r0, batch), SLAB_STATE:2 * SLAB_STATE]
        s_scr[pl.ds(r0, batch), 0:SLAB_STATE] = xr
        s_scr[pl.ds(r0, batch), SLAB_STATE:2 * SLAB_STATE] = xi
        return ar * xr - ai * xi + sr, ar * xi + ai * xr + si

    xr, xi = lax.fori_loop(0, rows // batch, step, (xr_scr[...], xi_scr[...]))
    xr_scr[...] = xr
    xi_scr[...] = xi

    y = _dot(ucat_scr[...], mt_scr[...]) + lax.dot_general(
        s_scr[...].astype(_BF16), mct_scr[...], (((1,), (1,)), ((), ())), preferred_element_type=_F32)
    for st in range(t):
        y_ref[0, pl.ds(st, rows, stride=t), :] = y[:, st * LANES:(st + 1) * LANES]


def _ssm(u, wc, cct, kc, ar, ai, layer, batch):
    nslab, tok_rows, _ = u.shape
    blk = SSM_CHUNK_BLOCK * batch * SSM_CHUNK
    rows = SSM_CHUNK_BLOCK * batch
    width = SSM_CHUNK * LANES
    io = pl.BlockSpec((1, blk, LANES), lambda q, ci: (q, ci, 0))
    par = lambda *shape: pl.BlockSpec((None, 1) + shape, lambda q, ci: (layer, q) + (0,) * len(shape))
    return pl.pallas_call(
        functools.partial(_ssm_kernel, batch=batch),
        grid=(nslab, tok_rows // blk),
        in_specs=[io, par(SSM_CHUNK, SSM_GROUP, 2 * SLAB_STATE), par(SSM_CHUNK, SSM_GROUP, 2 * SLAB_STATE),
                  par(SSM_CHUNK, SSM_GROUP, LANES), par(1, SLAB_STATE), par(1, SLAB_STATE)],
        out_specs=io,
        out_shape=jax.ShapeDtypeStruct(u.shape, _F32),
        scratch_shapes=[
            pltpu.VMEM((width, 2 * SLAB_STATE), _BF16),
            pltpu.VMEM((width, width), _BF16),
            pltpu.VMEM((width, 2 * SLAB_STATE), _BF16),
            pltpu.VMEM((rows, width), _BF16),
            pltpu.VMEM((rows, 2 * SLAB_STATE), _F32),
            pltpu.VMEM((batch, SLAB_STATE), _F32),
            pltpu.VMEM((batch, SLAB_STATE), _F32),
        ],
        name="ssm",
        compiler_params=pltpu.CompilerParams(
            dimension_semantics=("arbitrary", "arbitrary"), vmem_limit_bytes=VMEM_LIMIT),
    )(u, wc, cct, kc, ar, ai)


def _ssm_operators(a_re, a_im, b_re, b_im, c_re, c_im, d, log_dt):
    t = SSM_CHUNK
    p, hh = SSM_STATE, SSM_GROUP
    depth = a_re.shape[0]
    a_re, a_im = a_re.astype(_F32)[:, None], a_im.astype(_F32)[:, None]
    dt = jnp.exp(log_dt.astype(_F32))[:, None, :, None]
    n = jnp.arange(t + 1, dtype=_F32)[None, :, None, None]
    mag = jnp.exp(a_re * dt * n)
    ang = a_im * dt * n
    pw_re, pw_im = mag * jnp.cos(ang), mag * jnp.sin(ang)
    nr, ni = pw_re[:, 1:2] - 1.0, pw_im[:, 1:2]
    den = a_re * a_re + a_im * a_im
    f_re, f_im = (nr * a_re + ni * a_im) / den, (ni * a_re - nr * a_im) / den
    b_re, b_im = b_re.astype(_F32)[:, None], b_im.astype(_F32)[:, None]
    bb_re = f_re[..., None] * b_re - f_im[..., None] * b_im
    bb_im = f_re[..., None] * b_im + f_im[..., None] * b_re

    rev = t - 1 - jnp.arange(t)
    rev_re, rev_im = pw_re[:, rev, :, :, None], pw_im[:, rev, :, :, None]
    w = jnp.stack([rev_re * bb_re - rev_im * bb_im, rev_re * bb_im + rev_im * bb_re], axis=1)
    w = w.reshape(depth, 2, t, SSM_SLABS, SLAB_GROUPS, p, hh)
    wc = w.transpose(0, 3, 2, 6, 1, 4, 5).reshape(depth, SSM_SLABS, t, hh, 2 * SLAB_STATE)

    c_re, c_im = c_re.astype(_F32)[:, None], c_im.astype(_F32)[:, None]
    pr, pi = pw_re[:, :, :, None, :], pw_im[:, :, :, None, :]
    cp_re, cp_im = c_re * pr - c_im * pi, c_re * pi + c_im * pr
    cm = jnp.stack([cp_re[:, 1:], -cp_im[:, 1:]], axis=1)
    cm = cm.reshape(depth, 2, t, SSM_SLABS, SLAB_GROUPS, hh, p)
    cct = cm.transpose(0, 3, 2, 5, 1, 4, 6).reshape(depth, SSM_SLABS, t, hh, 2 * SLAB_STATE)

    hi = lax.Precision.HIGHEST
    kern = (jnp.einsum("ydghp,ygpk->ydghk", cp_re[:, :t], bb_re[:, 0], precision=hi)
            - jnp.einsum("ydghp,ygpk->ydghk", cp_im[:, :t], bb_im[:, 0], precision=hi))
    skip = jnp.eye(hh, dtype=_F32) * d.astype(_F32).reshape(depth, 1, -1, hh, 1)
    kern = jnp.concatenate([kern[:, :1] + skip, kern[:, 1:]], axis=1)
    kc = kern.reshape(depth, t, SSM_SLABS, SLAB_GROUPS, hh, hh).transpose(0, 2, 1, 5, 3, 4)
    kc = kc.reshape(depth, SSM_SLABS, t, hh, LANES)

    ar = pw_re[:, t].reshape(depth, SSM_SLABS, 1, SLAB_STATE)
    ai = pw_im[:, t].reshape(depth, SSM_SLABS, 1, SLAB_STATE)
    return wc, cct, kc, ar, ai


def _mixer_out_kernel(x_ref, part_ref, g2_ref, y_ref, wglu_ref, wso_ref, wmix_ref, nf_ref,
                      wfi_ref, wfo_ref, nlast_ref, out_ref, *, final):
    tm = x_ref.shape[1]
    y = jnp.concatenate([y_ref[sl].reshape(tm, LANES) for sl in range(SSM_SLABS)], axis=1)
    ys = 0.5 * y * (1.0 + jnp.tanh(math.sqrt(2.0 / math.pi) * (y + 0.044715 * (y * y * y))))
    glu = ys * _sigmoid(_dot(ys.astype(_BF16), wglu_ref[...]))
    y_ssm = _dot(glu.astype(_BF16), wso_ref[...])
    merged = part_ref[0].astype(_F32) + g2_ref[0].astype(_F32) * y_ssm
    x1 = x_ref[0] + _dot(merged.astype(_BF16), wmix_ref[...])

    h = _rmsnorm(x1, nf_ref[...]).astype(_BF16)
    gt = _dot(h, wfi_ref[:, 0:FFN_HIDDEN])
    up = _dot(h, wfi_ref[:, FFN_HIDDEN:2 * FFN_HIDDEN])
    act = (gt * _sigmoid(gt) * up).astype(_BF16)
    x2 = x1 + _dot(act, wfo_ref[...])
    if final:
        x2 = _rmsnorm(x2, nlast_ref[...])
    out_ref[0] = x2


def _mixer_out(x, part, g2, y, wglu, wso, wmix, nf, wfi, wfo, nlast, layer, final):
    b, l, _ = x.shape
    tm = TOKEN_TILE
    nchunk = tm // SSM_CHUNK
    tok = pl.BlockSpec((1, tm, D_MODEL), lambda bi, li: (bi, li, 0))
    in_specs = [
        tok, tok, tok,
        _ssm_io_spec(nchunk),
        _layer_spec((SSM_WIDTH, SSM_WIDTH), layer),
        _layer_spec((SSM_WIDTH, D_MODEL), layer),
        _layer_spec((D_MODEL, D_MODEL), layer),
        _layer_spec((1, D_MODEL), layer),
        _layer_spec((D_MODEL, 2 * FFN_HIDDEN), layer),
        _layer_spec((FFN_HIDDEN, D_MODEL), layer),
        _const_spec((1, D_MODEL)),
    ]
    return pl.pallas_call(
        functools.partial(_mixer_out_kernel, final=final),
        grid=(b, l // tm), in_specs=in_specs, out_specs=tok,
        out_shape=jax.ShapeDtypeStruct((b, l, D_MODEL), _F32),
        name="mixer_out",
        compiler_params=pltpu.CompilerParams(
            dimension_semantics=("arbitrary", "arbitrary"), vmem_limit_bytes=VMEM_LIMIT),
    )(x, part, g2, y, wglu, wso, wmix, nf, wfi, wfo, nlast)


def _rope_tables(seq_len):
    half = ROT_DIM // 2
    pos = jnp.arange(seq_len, dtype=_F32)
    inv_freq = ROPE_THETA ** (-jnp.arange(0, ROT_DIM, 2, dtype=_F32) / ROT_DIM)
    ang = pos[:, None] * inv_freq[None, :]
    cos, sin = jnp.cos(ang), jnp.sin(ang)
    ones = jnp.ones((seq_len, HEAD_DIM - ROT_DIM), _F32)
    zeros_h = jnp.zeros((seq_len, half), _F32)
    zeros_r = jnp.zeros((seq_len, HEAD_DIM - ROT_DIM), _F32)
    cos_h = jnp.concatenate([cos, cos, ones], axis=1)
    sa_h = jnp.concatenate([-sin, zeros_h, zeros_r], axis=1)
    sb_h = jnp.concatenate([zeros_h, sin, zeros_r], axis=1)
    rep = LANES // HEAD_DIM
    return jnp.tile(cos_h, (1, rep)), jnp.tile(sa_h, (1, rep)), jnp.tile(sb_h, (1, rep))


def kernel(x, norm_mix, w_in, b_gate, attn_sinks, w_attn_o, conv_w, w_conv_o, ssm_a_re, ssm_a_im,
           ssm_b_re, ssm_b_im, ssm_c_re, ssm_c_im, ssm_d, ssm_log_dt, w_ssm_glu, w_ssm_o, w_mix_o,
           norm_ffn, w_ffn_in, w_ffn_out, norm_final):
    b, l, d_model = x.shape
    depth = w_in.shape[0]
    assert d_model == D_MODEL and l % TOKEN_TILE == 0 and b % SUBLANES == 0
    assert l % (SSM_CHUNK * SSM_CHUNK_BLOCK) == 0
    assert w_in.shape[2] == IN_COLS and w_ffn_out.shape[1] == FFN_HIDDEN
    cos_t, sa_t, sb_t = _rope_tables(l)
    f32 = lambda p: p.astype(_F32)
    bf = lambda w: w.astype(_BF16)
    norm_mix, norm_ffn = f32(norm_mix)[:, None, :], f32(norm_ffn)[:, None, :]
    b_gate, conv_w, attn_sinks = f32(b_gate)[:, None, :], f32(conv_w), f32(attn_sinks)
    w_in, w_attn_o, w_conv_o = bf(w_in), bf(w_attn_o), bf(w_conv_o)
    w_ssm_glu, w_ssm_o, w_mix_o = bf(w_ssm_glu), bf(w_ssm_o), bf(w_mix_o)
    w_ffn_in, w_ffn_out = bf(w_ffn_in), bf(w_ffn_out)
    nlast = f32(norm_final).reshape(1, D_MODEL)
    wc, cct, kc, ar, ai = _ssm_operators(
        ssm_a_re, ssm_a_im, ssm_b_re, ssm_b_im, ssm_c_re, ssm_c_im, ssm_d, ssm_log_dt)
    x = f32(x)
    for i in range(depth):
        part, g2, u = _mixer_in(x, attn_sinks[i], norm_mix, w_in, b_gate, cos_t, sa_t, sb_t, conv_w,
                                w_attn_o, w_conv_o, i)
        y = _ssm(u.reshape(SSM_SLABS, l * b, LANES), wc, cct, kc, ar, ai, i, b).reshape(u.shape)
        x = _mixer_out(x, part, g2, y, w_ssm_glu, w_ssm_o, w_mix_o, norm_ffn, w_ffn_in, w_ffn_out,
                       nlast, i, i == depth - 1)
    return x
```

```python
import functools
import math

import jax
import jax.numpy as jnp
from jax import lax
from jax.experimental import pallas as pl
from jax.experimental.pallas import tpu as pltpu

D_MODEL = 1024
N_Q_HEADS = 8
N_KV_HEADS = 2
HEAD_DIM = 64
Q_GROUP = N_Q_HEADS // N_KV_HEADS
WINDOW = 128
ROPE_THETA = 500000.0
ROT_DIM = HEAD_DIM // 4
ATTN_WIDTH = N_Q_HEADS * HEAD_DIM
KV_WIDTH = N_KV_HEADS * HEAD_DIM
NEG_INF = -1e30
CONV_WIDTH = D_MODEL // 2
CONV_K = 3
SSM_WIDTH = D_MODEL // 2
SSM_GROUP = 16
SSM_GROUPS = SSM_WIDTH // SSM_GROUP
SSM_STATE = 64
GATE_WIDTH = 3 * D_MODEL
FFN_HIDDEN = 2816
NORM_EPS = 1e-6

_Q0 = 0
_KV0 = _Q0 + ATTN_WIDTH
_CB0 = _KV0 + 2 * KV_WIDTH
_CC0 = _CB0 + CONV_WIDTH
_CX0 = _CC0 + CONV_WIDTH
_U0 = _CX0 + CONV_WIDTH
_G0 = _U0 + SSM_WIDTH
IN_COLS = _G0 + GATE_WIDTH

LANES = 128
SUBLANES = 8
BF16_ROWS = 2 * SUBLANES
SSM_CHUNK = 8
SSM_SLABS = SSM_WIDTH // LANES
SLAB_GROUPS = LANES // SSM_GROUP
SLAB_STATE = SLAB_GROUPS * SSM_STATE
TOKEN_TILE = 512
SSM_CHUNK_BLOCK = 64
VMEM_LIMIT = 56 * 1024 * 1024

_BF16 = jnp.bfloat16
_F32 = jnp.float32


def _dot(a, b):
    return jnp.dot(a, b, preferred_element_type=_F32)


def _sigmoid(x):
    return 0.5 * jnp.tanh(0.5 * x) + 0.5


def _rmsnorm(x, w):
    ms = jnp.mean(x * x, axis=-1, keepdims=True)
    return x * lax.rsqrt(ms + NORM_EPS) * w


def _const_spec(shape):
    return pl.BlockSpec(shape, lambda *_: (0,) * len(shape), pipeline_mode=pl.Buffered(1))


def _layer_spec(shape, layer):
    return pl.BlockSpec((None,) + shape, lambda *_: (layer,) + (0,) * len(shape), pipeline_mode=pl.Buffered(1))


def _cast_plan(weights, layer, grid):
    nsteps = grid[0] * grid[1]
    in_specs, out_specs, out_shapes = [], [], []
    for w in weights:
        _, rows, cols = w.shape
        ncol = next(n for n in (1, 2, 4, 8)
                    if (rows * n) % (nsteps * BF16_ROWS) == 0 and cols % (n * LANES) == 0)
        nrow = nsteps // ncol
        block = (rows // nrow, cols // ncol)

        def index(bi, li, nrow=nrow):
            step = bi * grid[1] + li
            return step % nrow, step // nrow

        in_specs.append(pl.BlockSpec((None,) + block, lambda bi, li, index=index: (layer,) + index(bi, li)))
        out_specs.append(pl.BlockSpec(block, index))
        out_shapes.append(jax.ShapeDtypeStruct((rows, cols), _BF16))
    return in_specs, out_specs, out_shapes


def _with_casts(body, n_in, n_out, n_cast):
    def kernel(*refs):
        ins, rest = refs[:n_in], refs[n_in:]
        cast_src, rest = rest[:n_cast], rest[n_cast:]
        outs, rest = rest[:n_out], rest[n_out:]
        cast_dst, scratch = rest[:n_cast], rest[n_cast:]
        body(*ins, *outs, *scratch)
        for src, dst in zip(cast_src, cast_dst):
            dst[...] = src[...].astype(_BF16)
    return kernel


def _ssm_io_spec(nchunk):
    return pl.BlockSpec((SSM_SLABS, nchunk, None, SSM_CHUNK, LANES), lambda bi, li: (0, li, bi, 0, 0))


def _mixer_in_kernel(sinks_ref, x_ref, nw_ref, win_ref, bg_ref, cos_ref, sa_ref, sb_ref, cw_ref,
                     wao_ref, wco_ref, part_ref, g2_ref, u_ref,
                     kv_scr, z_scr, attn_scr, yc_scr, acc_scr, g0_scr):
    tm = x_ref.shape[1]
    li = pl.program_id(1)

    @pl.when(li == 0)
    def _():
        kv_scr[:, 0:WINDOW, :] = jnp.zeros((8, WINDOW, LANES), _BF16)
        z_scr[0:SUBLANES, :] = jnp.zeros((SUBLANES, CONV_WIDTH), _F32)

    x = x_ref[0]
    h = _rmsnorm(x, nw_ref[...]).astype(_BF16)

    cos = cos_ref[...]
    sa = sa_ref[...]
    sb = sb_ref[...]

    def rope(t):
        return (t * cos + pltpu.roll(t, LANES - ROT_DIM // 2, axis=1) * sa
                + pltpu.roll(t, ROT_DIM // 2, axis=1) * sb)

    kv = _dot(h, win_ref[:, _KV0:_KV0 + 2 * KV_WIDTH])
    lane = lax.broadcasted_iota(jnp.int32, (tm, LANES), 1)
    lo = lane < HEAD_DIM
    for kind in range(2):
        t = kv[:, kind * LANES:(kind + 1) * LANES]
        if kind == 0:
            t = rope(t)
        t_sw = pltpu.roll(t, HEAD_DIM, axis=1)
        zero = jnp.zeros_like(t)
        variants = (jnp.where(lo, t, zero), jnp.where(lo, zero, t_sw),
                    jnp.where(lo, t_sw, zero), jnp.where(lo, zero, t))
        for n, val in enumerate(variants):
            kv_scr[kind * 4 + n, WINDOW:WINDOW + tm, :] = val.astype(_BF16)

    q = _dot(h, win_ref[:, _Q0:_Q0 + ATTN_WIDTH])
    row = lax.broadcasted_iota(jnp.int32, (WINDOW, 2 * WINDOW), 0)
    col = lax.broadcasted_iota(jnp.int32, (WINDOW, 2 * WINDOW), 1)
    band_ok = (col > row) & (col <= row + WINDOW)
    first_ok = band_ok & (col >= jnp.where(li > 0, 0, WINDOW))

    def gate(n):
        c0 = _G0 + n * D_MODEL
        return _sigmoid(_dot(h, win_ref[:, c0:c0 + D_MODEL]) + bg_ref[:, n * D_MODEL:(n + 1) * D_MODEL])

    def conv_in():
        cb = _dot(h, win_ref[:, _CB0:_CB0 + CONV_WIDTH])
        cc = _dot(h, win_ref[:, _CC0:_CC0 + CONV_WIDTH])
        cx = _dot(h, win_ref[:, _CX0:_CX0 + CONV_WIDTH])
        z = cc * cx
        z_scr[SUBLANES:SUBLANES + tm, :] = z
        z1 = z_scr[SUBLANES - 1:SUBLANES - 1 + tm, :]
        z2 = z_scr[SUBLANES - 2:SUBLANES - 2 + tm, :]
        conv = cw_ref[0:1, :] * z2 + cw_ref[1:2, :] * z1 + cw_ref[2:3, :] * z
        yc_scr[...] = (cb * conv).astype(_BF16)

    def conv_out():
        acc_scr[...] = gate(1) * _dot(yc_scr[...], wco_ref[...])

    def attn_gate_and_ssm_in():
        g0_scr[...] = gate(0)
        u = _dot(h, win_ref[:, _U0:_U0 + SSM_WIDTH])
        nchunk = tm // SSM_CHUNK
        for sl in range(SSM_SLABS):
            u_ref[sl] = u[:, sl * LANES:(sl + 1) * LANES].reshape(nchunk, SSM_CHUNK, LANES)

    def ssm_gate():
        g2_ref[0] = gate(2).astype(_BF16)

    side_work = (conv_in, conv_out, attn_gate_and_ssm_in, ssm_gate)

    for m in range(N_Q_HEADS // 2):
        side_work[m]()
        qp = (rope(q[:, m * LANES:(m + 1) * LANES]) * (HEAD_DIM ** -0.5)).astype(_BF16)
        j = (2 * m) // Q_GROUP
        for r in range(tm // WINDOW):
            qb = qp[r * WINDOW:(r + 1) * WINDOW]
            mask = first_ok if r == 0 else band_ok
            o = None
            for half in range(2):
                sink = sinks_ref[2 * m + half]
                kb = kv_scr[j * 2 + half, r * WINDOW:(r + 2) * WINDOW, :]
                vb = kv_scr[4 + j * 2 + half, r * WINDOW:(r + 2) * WINDOW, :]
                s = lax.dot_general(qb, kb, (((1,), (1,)), ((), ())), preferred_element_type=_F32)
                s = jnp.where(mask, s, NEG_INF)
                mx = jnp.maximum(jnp.max(s, axis=-1, keepdims=True), sink)
                p = jnp.exp(s - mx)
                denom = jnp.sum(p, axis=-1, keepdims=True) + jnp.exp(sink - mx)
                oh = _dot(p.astype(_BF16), vb) * (1.0 / denom)
                o = oh if o is None else o + oh
            attn_scr[r * WINDOW:(r + 1) * WINDOW, m * LANES:(m + 1) * LANES] = o.astype(_BF16)

    y_attn = _dot(attn_scr[...], wao_ref[...])
    part_ref[0] = (g0_scr[...] * y_attn + acc_scr[...]).astype(_BF16)

    kv_scr[:, 0:WINDOW, :] = kv_scr[:, tm:tm + WINDOW, :]
    z_scr[0:SUBLANES, :] = z_scr[tm:tm + SUBLANES, :]


def _mixer_in(x, sinks, nw, win, bg, cos_t, sa_t, sb_t, cw, wao, wco, layer, cast_weights):
    b, l, _ = x.shape
    tm = TOKEN_TILE
    nchunk = tm // SSM_CHUNK
    grid = (b, l // tm)
    in_specs = [
        pl.BlockSpec(memory_space=pltpu.SMEM),
        pl.BlockSpec((1, tm, D_MODEL), lambda bi, li: (bi, li, 0)),
        _layer_spec((1, D_MODEL), layer),
        _const_spec((D_MODEL, IN_COLS)),
        _layer_spec((1, GATE_WIDTH), layer),
        pl.BlockSpec((tm, LANES), lambda bi, li: (li, 0)),
        pl.BlockSpec((tm, LANES), lambda bi, li: (li, 0)),
        pl.BlockSpec((tm, LANES), lambda bi, li: (li, 0)),
        _layer_spec((CONV_K, CONV_WIDTH), layer),
        _const_spec((ATTN_WIDTH, D_MODEL)),
        _const_spec((CONV_WIDTH, D_MODEL)),
    ]
    out_specs = [
        pl.BlockSpec((1, tm, D_MODEL), lambda bi, li: (bi, li, 0)),
        pl.BlockSpec((1, tm, D_MODEL), lambda bi, li: (bi, li, 0)),
        _ssm_io_spec(nchunk),
    ]
    out_shape = [
        jax.ShapeDtypeStruct((b, l, D_MODEL), _BF16),
        jax.ShapeDtypeStruct((b, l, D_MODEL), _BF16),
        jax.ShapeDtypeStruct((SSM_SLABS, l // SSM_CHUNK, b, SSM_CHUNK, LANES), _F32),
    ]
    scratch = [
        pltpu.VMEM((8, WINDOW + tm, LANES), _BF16),
        pltpu.VMEM((SUBLANES + tm, CONV_WIDTH), _F32),
        pltpu.VMEM((tm, ATTN_WIDTH), _BF16),
        pltpu.VMEM((tm, CONV_WIDTH), _BF16),
        pltpu.VMEM((tm, D_MODEL), _F32),
        pltpu.VMEM((tm, D_MODEL), _F32),
    ]
    cast_in, cast_out, cast_shape = _cast_plan(cast_weights, layer, grid)
    return pl.pallas_call(
        _with_casts(_mixer_in_kernel, len(in_specs), len(out_specs), len(cast_weights)),
        grid=grid, in_specs=in_specs + cast_in, out_specs=out_specs + cast_out,
        out_shape=out_shape + cast_shape, scratch_shapes=scratch, name="mixer_in",
        compiler_params=pltpu.CompilerParams(
            dimension_semantics=("arbitrary", "arbitrary"), vmem_limit_bytes=VMEM_LIMIT),
    )(sinks, x, nw, win, bg, cos_t, sa_t, sb_t, cw, wao, wco, *cast_weights)


def _ssm_kernel(u_ref, wc_ref, cct_ref, kc_ref, ar_ref, ai_ref, y_ref,
                mb_scr, mt_scr, mct_scr, ucat_scr, s_scr, xr_scr, xi_scr, *, batch):
    t = SSM_CHUNK
    rows = u_ref.shape[1] // t

    @pl.when(pl.program_id(1) == 0)
    def _():
        chan_bits, state_bits = SSM_GROUP.bit_length() - 1, SSM_STATE.bit_length() - 1
        r_grp = lax.broadcasted_iota(jnp.int32, (LANES, 2 * SLAB_STATE), 0) >> chan_bits
        c_grp = (lax.broadcasted_iota(jnp.int32, (LANES, 2 * SLAB_STATE), 1) & (SLAB_STATE - 1)) >> state_bits
        same_state = r_grp == c_grp

        def expand(tbl):
            return jnp.where(same_state, jnp.tile(tbl, (SLAB_GROUPS, 1)), 0.0)

        r_grp = lax.broadcasted_iota(jnp.int32, (LANES, LANES), 0) >> chan_bits
        c_grp = lax.broadcasted_iota(jnp.int32, (LANES, LANES), 1) >> chan_bits
        same_chan = r_grp == c_grp
        zero_blk = jnp.zeros((LANES, LANES), _BF16)
        blks = []
        for st in range(t):
            rs = slice(st * LANES, (st + 1) * LANES)
            mb_scr[rs, :] = expand(wc_ref[0, st]).astype(_BF16)
            mct_scr[rs, :] = expand(cct_ref[0, st]).astype(_BF16)
            blks.append(jnp.where(same_chan, jnp.tile(kc_ref[0, st], (SLAB_GROUPS, 1)), 0.0).astype(_BF16))
        for s_in in range(t):
            for s_out in range(t):
                mt_scr[s_in * LANES:(s_in + 1) * LANES, s_out * LANES:(s_out + 1) * LANES] = (
                    blks[s_out - s_in] if s_out >= s_in else zero_blk)
        xr_scr[...] = jnp.zeros_like(xr_scr)
        xi_scr[...] = jnp.zeros_like(xi_scr)

    for st in range(t):
        ucat_scr[:, st * LANES:(st + 1) * LANES] = u_ref[0, pl.ds(st, rows, stride=t), :].astype(_BF16)

    s_scr[...] = _dot(ucat_scr[...], mb_scr[...])

    ar = jnp.broadcast_to(ar_ref[0], (batch, SLAB_STATE))
    ai = jnp.broadcast_to(ai_ref[0], (batch, SLAB_STATE))

    def step(c, carry):
        xr, xi = carry
        r0 = pl.multiple_of(c * batch, batch)
        sr = s_scr[pl.ds(r0, batch), 0:SLAB_STATE]
        si = s_scr[pl.ds(r0, batch), SLAB_STATE:2 * SLAB_STATE]
        s_scr[pl.ds(r0, batch), 0:SLAB_STATE] = xr
        s_scr[pl.ds(r0, batch), SLAB_STATE:2 * SLAB_STATE] = xi
        return ar * xr - ai * xi + sr, ar * xi + ai * xr + si

    xr, xi = lax.fori_loop(0, rows // batch, step, (xr_scr[...], xi_scr[...]))
    xr_scr[...] = xr
    xi_scr[...] = xi

    y = _dot(ucat_scr[...], mt_scr[...]) + lax.dot_general(
        s_scr[...].astype(_BF16), mct_scr[...], (((1,), (1,)), ((), ())), preferred_element_type=_F32)
    for st in range(t):
        y_ref[0, pl.ds(st, rows, stride=t), :] = y[:, st * LANES:(st + 1) * LANES]


def _ssm(u, wc, cct, kc, ar, ai, layer, batch):
    nslab, tok_rows, _ = u.shape
    blk = SSM_CHUNK_BLOCK * batch * SSM_CHUNK
    rows = SSM_CHUNK_BLOCK * batch
    width = SSM_CHUNK * LANES
    io = pl.BlockSpec((1, blk, LANES), lambda q, ci: (q, ci, 0))
    par = lambda *shape: pl.BlockSpec((None, 1) + shape, lambda q, ci: (layer, q) + (0,) * len(shape))
    return pl.pallas_call(
        functools.partial(_ssm_kernel, batch=batch),
        grid=(nslab, tok_rows // blk),
        in_specs=[io, par(SSM_CHUNK, SSM_GROUP, 2 * SLAB_STATE), par(SSM_CHUNK, SSM_GROUP, 2 * SLAB_STATE),
                  par(SSM_CHUNK, SSM_GROUP, LANES), par(1, SLAB_STATE), par(1, SLAB_STATE)],
        out_specs=io,
        out_shape=jax.ShapeDtypeStruct(u.shape, _F32),
        scratch_shapes=[
            pltpu.VMEM((width, 2 * SLAB_STATE), _BF16),
            pltpu.VMEM((width, width), _BF16),
            pltpu.VMEM((width, 2 * SLAB_STATE), _BF16),
            pltpu.VMEM((rows, width), _BF16),
            pltpu.VMEM((rows, 2 * SLAB_STATE), _F32),
            pltpu.VMEM((batch, SLAB_STATE), _F32),
            pltpu.VMEM((batch, SLAB_STATE), _F32),
        ],
        name="ssm",
        compiler_params=pltpu.CompilerParams(
            dimension_semantics=("arbitrary", "arbitrary"), vmem_limit_bytes=VMEM_LIMIT),
    )(u, wc, cct, kc, ar, ai)


def _ssm_operators(a_re, a_im, b_re, b_im, c_re, c_im, d, log_dt):
    t = SSM_CHUNK
    p, hh, nq, ng = SSM_STATE, SSM_GROUP, SSM_SLABS, SLAB_GROUPS
    depth = a_re.shape[0]
    hi = lax.Precision.HIGHEST
    a_re, a_im = a_re.astype(_F32), a_im.astype(_F32)
    dt = jnp.exp(log_dt.astype(_F32))[:, :, None]
    b_re, b_im = b_re.astype(_F32), b_im.astype(_F32)
    c_re, c_im = c_re.astype(_F32), c_im.astype(_F32)

    def powers(ar_, ai_, dt_, n):
        mag, ang = jnp.exp(ar_ * dt_ * n), ai_ * dt_ * n
        return mag * jnp.cos(ang), mag * jnp.sin(ang)

    def zoh(ar_, ai_, dt_):
        lr, li_ = powers(ar_, ai_, dt_, 1.0)
        nr, den = lr - 1.0, ar_ * ar_ + ai_ * ai_
        return (nr * ar_ + li_ * ai_) / den, (li_ * ar_ - nr * ai_) / den

    steps = jnp.arange(t, dtype=_F32)[:, None, None]
    pr, pi = powers(a_re[:, :, None, None], a_im[:, :, None, None], dt[:, :, None, None], steps)
    cr, ci = c_re[:, :, None], c_im[:, :, None]
    lhs = jnp.concatenate([cr * pr - ci * pi, -(cr * pi + ci * pr)], axis=-1)
    f_re, f_im = zoh(a_re, a_im, dt)
    rhs = jnp.concatenate([f_re[..., None] * b_re - f_im[..., None] * b_im,
                           f_re[..., None] * b_im + f_im[..., None] * b_re], axis=-2)
    kern = jnp.einsum("ygmc,ygck->ygmk", lhs.reshape(depth, -1, t * hh, 2 * p), rhs, precision=hi)
    kern = kern.reshape(depth, -1, t, hh, hh)
    skip = jnp.eye(hh, dtype=_F32) * d.astype(_F32).reshape(depth, -1, hh, 1)
    kern = jnp.concatenate([kern[:, :, :1] + skip[:, :, None], kern[:, :, 1:]], axis=2)
    kc = kern.reshape(depth, nq, ng, t, hh, hh).transpose(0, 1, 3, 5, 2, 4).reshape(depth, nq, t, hh, LANES)

    lane = lambda v: v.reshape(depth, nq, 1, ng * p)
    la_re, la_im = lane(a_re), lane(a_im)
    ldt = lane(jnp.broadcast_to(dt, a_re.shape))
    lf_re, lf_im = zoh(la_re, la_im, ldt)
    to_rows = lambda v, axes: v.reshape(depth, nq, ng, *v.shape[2:]).transpose(axes).reshape(depth, nq, hh, ng * p)
    lb_re, lb_im = to_rows(b_re, (0, 1, 4, 2, 3)), to_rows(b_im, (0, 1, 4, 2, 3))
    lc_re, lc_im = to_rows(c_re, (0, 1, 3, 2, 4)), to_rows(c_im, (0, 1, 3, 2, 4))
    bb_re, bb_im = lf_re * lb_re - lf_im * lb_im, lf_re * lb_im + lf_im * lb_re

    rev = (t - 1 - jnp.arange(t, dtype=_F32))[:, None, None]
    rr, ri = powers(la_re[:, :, None], la_im[:, :, None], ldt[:, :, None], rev)
    wc = jnp.concatenate([rr * bb_re[:, :, None] - ri * bb_im[:, :, None],
                          rr * bb_im[:, :, None] + ri * bb_re[:, :, None]], axis=-1)
    fwd = (1.0 + jnp.arange(t, dtype=_F32))[:, None, None]
    fr, fi = powers(la_re[:, :, None], la_im[:, :, None], ldt[:, :, None], fwd)
    cct = jnp.concatenate([lc_re[:, :, None] * fr - lc_im[:, :, None] * fi,
                           -(lc_re[:, :, None] * fi + lc_im[:, :, None] * fr)], axis=-1)
    ar, ai = powers(la_re, la_im, ldt, float(t))
    return wc, cct, kc, ar, ai


def _mixer_out_kernel(x_ref, part_ref, g2_ref, y_ref, wglu_ref, wso_ref, wmix_ref, nf_ref,
                      wfi_ref, wfo_ref, nlast_ref, out_ref, *, final):
    tm = x_ref.shape[1]
    y = jnp.concatenate([y_ref[sl].reshape(tm, LANES) for sl in range(SSM_SLABS)], axis=1)
    ys = 0.5 * y * (1.0 + jnp.tanh(math.sqrt(2.0 / math.pi) * (y + 0.044715 * (y * y * y))))
    glu = ys * _sigmoid(_dot(ys.astype(_BF16), wglu_ref[...]))
    y_ssm = _dot(glu.astype(_BF16), wso_ref[...])
    merged = part_ref[0].astype(_F32) + g2_ref[0].astype(_F32) * y_ssm
    x1 = x_ref[0] + _dot(merged.astype(_BF16), wmix_ref[...])

    h = _rmsnorm(x1, nf_ref[...]).astype(_BF16)
    gt = _dot(h, wfi_ref[:, 0:FFN_HIDDEN])
    up = _dot(h, wfi_ref[:, FFN_HIDDEN:2 * FFN_HIDDEN])
    act = (gt * _sigmoid(gt) * up).astype(_BF16)
    x2 = x1 + _dot(act, wfo_ref[...])
    if final:
        x2 = _rmsnorm(x2, nlast_ref[...])
    out_ref[0] = x2


def _mixer_out(x, part, g2, y, wglu, wso, wmix, nf, wfi, wfo, nlast, layer, final, cast_weights):
    b, l, _ = x.shape
    tm = TOKEN_TILE
    nchunk = tm // SSM_CHUNK
    grid = (b, l // tm)
    tok = pl.BlockSpec((1, tm, D_MODEL), lambda bi, li: (bi, li, 0))
    in_specs = [
        tok, tok, tok,
        _ssm_io_spec(nchunk),
        _const_spec((SSM_WIDTH, SSM_WIDTH)),
        _const_spec((SSM_WIDTH, D_MODEL)),
        _const_spec((D_MODEL, D_MODEL)),
        _layer_spec((1, D_MODEL), layer),
        _const_spec((D_MODEL, 2 * FFN_HIDDEN)),
        _const_spec((FFN_HIDDEN, D_MODEL)),
        _const_spec((1, D_MODEL)),
    ]
    cast_in, cast_out, cast_shape = _cast_plan(cast_weights, layer + 1, grid)
    body = functools.partial(_mixer_out_kernel, final=final)
    return pl.pallas_call(
        _with_casts(body, len(in_specs), 1, len(cast_weights)),
        grid=grid, in_specs=in_specs + cast_in, out_specs=[tok] + cast_out,
        out_shape=[jax.ShapeDtypeStruct((b, l, D_MODEL), _F32)] + cast_shape,
        name="mixer_out",
        compiler_params=pltpu.CompilerParams(
            dimension_semantics=("arbitrary", "arbitrary"), vmem_limit_bytes=VMEM_LIMIT),
    )(x, part, g2, y, wglu, wso, wmix, nf, wfi, wfo, nlast, *cast_weights)


def _rope_tables(seq_len):
    half = ROT_DIM // 2
    pos = jnp.arange(seq_len, dtype=_F32)
    inv_freq = ROPE_THETA ** (-jnp.arange(0, ROT_DIM, 2, dtype=_F32) / ROT_DIM)
    ang = pos[:, None] * inv_freq[None, :]
    cos, sin = jnp.cos(ang), jnp.sin(ang)
    ones = jnp.ones((seq_len, HEAD_DIM - ROT_DIM), _F32)
    zeros_h = jnp.zeros((seq_len, half), _F32)
    zeros_r = jnp.zeros((seq_len, HEAD_DIM - ROT_DIM), _F32)
    cos_h = jnp.concatenate([cos, cos, ones], axis=1)
    sa_h = jnp.concatenate([-sin, zeros_h, zeros_r], axis=1)
    sb_h = jnp.concatenate([zeros_h, sin, zeros_r], axis=1)
    rep = LANES // HEAD_DIM
    return jnp.tile(cos_h, (1, rep)), jnp.tile(sa_h, (1, rep)), jnp.tile(sb_h, (1, rep))


def kernel(x, norm_mix, w_in, b_gate, attn_sinks, w_attn_o, conv_w, w_conv_o, ssm_a_re, ssm_a_im,
           ssm_b_re, ssm_b_im, ssm_c_re, ssm_c_im, ssm_d, ssm_log_dt, w_ssm_glu, w_ssm_o, w_mix_o,
           norm_ffn, w_ffn_in, w_ffn_out, norm_final):
    b, l, d_model = x.shape
    depth = w_in.shape[0]
    assert d_model == D_MODEL and l % TOKEN_TILE == 0 and b % SUBLANES == 0
    assert l % (SSM_CHUNK * SSM_CHUNK_BLOCK) == 0
    assert w_in.shape[2] == IN_COLS and w_ffn_out.shape[1] == FFN_HIDDEN
    cos_t, sa_t, sb_t = _rope_tables(l)
    f32 = lambda p: p.astype(_F32)
    norm_mix, norm_ffn = f32(norm_mix)[:, None, :], f32(norm_ffn)[:, None, :]
    b_gate, conv_w, attn_sinks = f32(b_gate)[:, None, :], f32(conv_w), f32(attn_sinks)
    nlast = f32(norm_final).reshape(1, D_MODEL)
    wc, cct, kc, ar, ai = _ssm_operators(
        ssm_a_re, ssm_a_im, ssm_b_re, ssm_b_im, ssm_c_re, ssm_c_im, ssm_d, ssm_log_dt)
    in_weights = tuple(f32(w) for w in (w_in, w_attn_o, w_conv_o))
    out_weights = tuple(f32(w) for w in (w_ssm_glu, w_ssm_o, w_mix_o, w_ffn_in, w_ffn_out))
    w_mixer_in = [w[0].astype(_BF16) for w in in_weights]
    x = f32(x)
    for i in range(depth):
        last = i == depth - 1
        win, wao, wco = w_mixer_in
        part, g2, u, *w_mixer_out = _mixer_in(
            x, attn_sinks[i], norm_mix, win, b_gate, cos_t, sa_t, sb_t, conv_w, wao, wco, i, out_weights)
        y = _ssm(u.reshape(SSM_SLABS, l * b, LANES), wc, cct, kc, ar, ai, i, b).reshape(u.shape)
        x, *w_mixer_in = _mixer_out(
            x, part, g2, y, *w_mixer_out[:3], norm_ffn, *w_mixer_out[3:], nlast, i, last,
            () if last else in_weights)
    return x
```

```python
import functools
import math

import jax
import jax.numpy as jnp
from jax import lax
from jax.experimental import pallas as pl
from jax.experimental.pallas import tpu as pltpu

D_MODEL = 1024
N_Q_HEADS = 8
N_KV_HEADS = 2
HEAD_DIM = 64
Q_GROUP = N_Q_HEADS // N_KV_HEADS
WINDOW = 128
ROPE_THETA = 500000.0
ROT_DIM = HEAD_DIM // 4
ATTN_WIDTH = N_Q_HEADS * HEAD_DIM
KV_WIDTH = N_KV_HEADS * HEAD_DIM
NEG_INF = -1e30
CONV_WIDTH = D_MODEL // 2
CONV_K = 3
SSM_WIDTH = D_MODEL // 2
SSM_GROUP = 16
SSM_GROUPS = SSM_WIDTH // SSM_GROUP
SSM_STATE = 64
GATE_WIDTH = 3 * D_MODEL
FFN_HIDDEN = 2816
NORM_EPS = 1e-6

_Q0 = 0
_KV0 = _Q0 + ATTN_WIDTH
_CB0 = _KV0 + 2 * KV_WIDTH
_CC0 = _CB0 + CONV_WIDTH
_CX0 = _CC0 + CONV_WIDTH
_U0 = _CX0 + CONV_WIDTH
_G0 = _U0 + SSM_WIDTH
IN_COLS = _G0 + GATE_WIDTH

LANES = 128
SUBLANES = 8
BF16_ROWS = 2 * SUBLANES
SSM_CHUNK = 8
SSM_SLABS = SSM_WIDTH // LANES
SLAB_GROUPS = LANES // SSM_GROUP
SLAB_STATE = SLAB_GROUPS * SSM_STATE
TOKEN_TILE = 512
SSM_CHUNK_BLOCK = 64
MATMUL_ROWS = 256
VMEM_LIMIT = 56 * 1024 * 1024

_BF16 = jnp.bfloat16
_F32 = jnp.float32


def _dot(a, b):
    m = a.shape[0]
    if m <= MATMUL_ROWS:
        return jnp.dot(a, b, preferred_element_type=_F32)
    return jnp.concatenate(
        [jnp.dot(a[i:i + MATMUL_ROWS], b, preferred_element_type=_F32) for i in range(0, m, MATMUL_ROWS)], axis=0)


def _sigmoid(x):
    return 0.5 * jnp.tanh(0.5 * x) + 0.5


def _rmsnorm(x, w):
    ms = jnp.mean(x * x, axis=-1, keepdims=True)
    return x * lax.rsqrt(ms + NORM_EPS) * w


def _const_spec(shape):
    return pl.BlockSpec(shape, lambda *_: (0,) * len(shape), pipeline_mode=pl.Buffered(1))


def _layer_spec(shape, layer):
    return pl.BlockSpec((None,) + shape, lambda *_: (layer,) + (0,) * len(shape), pipeline_mode=pl.Buffered(1))


def _cast_plan(weights, layer, grid):
    nsteps = grid[0] * grid[1]
    in_specs, out_specs, out_shapes = [], [], []
    for w in weights:
        _, rows, cols = w.shape
        ncol = next(n for n in (1, 2, 4, 8)
                    if (rows * n) % (nsteps * BF16_ROWS) == 0 and cols % (n * LANES) == 0)
        nrow = nsteps // ncol
        block = (rows // nrow, cols // ncol)

        def index(bi, li, nrow=nrow):
            step = bi * grid[1] + li
            return step % nrow, step // nrow

        in_specs.append(pl.BlockSpec((None,) + block, lambda bi, li, index=index: (layer,) + index(bi, li)))
        out_specs.append(pl.BlockSpec(block, index))
        out_shapes.append(jax.ShapeDtypeStruct((rows, cols), _BF16))
    return in_specs, out_specs, out_shapes


def _with_casts(body, n_in, n_out, n_cast):
    def kernel(*refs):
        ins, rest = refs[:n_in], refs[n_in:]
        cast_src, rest = rest[:n_cast], rest[n_cast:]
        outs, rest = rest[:n_out], rest[n_out:]
        cast_dst, scratch = rest[:n_cast], rest[n_cast:]
        body(*ins, *outs, *scratch)
        for src, dst in zip(cast_src, cast_dst):
            dst[...] = src[...].astype(_BF16)
    return kernel


def _ssm_io_spec(nchunk):
    return pl.BlockSpec((SSM_SLABS, nchunk, None, SSM_CHUNK, LANES), lambda bi, li: (0, li, bi, 0, 0))


def _mixer_in_kernel(sinks_ref, x_ref, nw_ref, win_ref, bg_ref, cos_ref, sa_ref, sb_ref, cw_ref,
                     wao_ref, wco_ref, part_ref, g2_ref, u_ref,
                     kv_scr, z_scr, attn_scr, yc_scr, acc_scr, g0_scr):
    tm = x_ref.shape[1]
    li = pl.program_id(1)

    @pl.when(li == 0)
    def _():
        kv_scr[:, 0:WINDOW, :] = jnp.zeros((8, WINDOW, LANES), _BF16)
        z_scr[0:SUBLANES, :] = jnp.zeros((SUBLANES, CONV_WIDTH), _F32)

    x = x_ref[0]
    h = _rmsnorm(x, nw_ref[...]).astype(_BF16)

    cos = cos_ref[...]
    sa = sa_ref[...]
    sb = sb_ref[...]

    def rope(t):
        return (t * cos + pltpu.roll(t, LANES - ROT_DIM // 2, axis=1) * sa
                + pltpu.roll(t, ROT_DIM // 2, axis=1) * sb)

    kv = _dot(h, win_ref[:, _KV0:_KV0 + 2 * KV_WIDTH])
    lane = lax.broadcasted_iota(jnp.int32, (tm, LANES), 1)
    lo = lane < HEAD_DIM
    for kind in range(2):
        t = kv[:, kind * LANES:(kind + 1) * LANES]
        if kind == 0:
            t = rope(t)
        t_sw = pltpu.roll(t, HEAD_DIM, axis=1)
        zero = jnp.zeros_like(t)
        variants = (jnp.where(lo, t, zero), jnp.where(lo, zero, t_sw),
                    jnp.where(lo, t_sw, zero), jnp.where(lo, zero, t))
        for n, val in enumerate(variants):
            kv_scr[kind * 4 + n, WINDOW:WINDOW + tm, :] = val.astype(_BF16)

    q = _dot(h, win_ref[:, _Q0:_Q0 + ATTN_WIDTH])
    row = lax.broadcasted_iota(jnp.int32, (WINDOW, 2 * WINDOW), 0)
    col = lax.broadcasted_iota(jnp.int32, (WINDOW, 2 * WINDOW), 1)
    band_ok = (col > row) & (col <= row + WINDOW)
    first_ok = band_ok & (col >= jnp.where(li > 0, 0, WINDOW))

    def gate(n):
        c0 = _G0 + n * D_MODEL
        return _sigmoid(_dot(h, win_ref[:, c0:c0 + D_MODEL]) + bg_ref[:, n * D_MODEL:(n + 1) * D_MODEL])


    cb = _dot(h, win_ref[:, _CB0:_CB0 + CONV_WIDTH])
    cc = _dot(h, win_ref[:, _CC0:_CC0 + CONV_WIDTH])
    cx = _dot(h, win_ref[:, _CX0:_CX0 + CONV_WIDTH])
    z = cc * cx
    z_scr[SUBLANES:SUBLANES + tm, :] = z
    z1 = z_scr[SUBLANES - 1:SUBLANES - 1 + tm, :]
    z2 = z_scr[SUBLANES - 2:SUBLANES - 2 + tm, :]
    conv = cw_ref[0:1, :] * z2 + cw_ref[1:2, :] * z1 + cw_ref[2:3, :] * z
    yc_scr[...] = (cb * conv).astype(_BF16)
    acc_scr[...] = gate(1) * _dot(yc_scr[...], wco_ref[...])

    g0_scr[...] = gate(0)
    u = _dot(h, win_ref[:, _U0:_U0 + SSM_WIDTH])
    nchunk = tm // SSM_CHUNK
    for sl in range(SSM_SLABS):
        u_ref[sl] = u[:, sl * LANES:(sl + 1) * LANES].reshape(nchunk, SSM_CHUNK, LANES)

    for m in range(N_Q_HEADS // 2):
        qp = (rope(q[:, m * LANES:(m + 1) * LANES]) * (HEAD_DIM ** -0.5)).astype(_BF16)
        j = (2 * m) // Q_GROUP
        for r in range(tm // WINDOW):
            qb = qp[r * WINDOW:(r + 1) * WINDOW]
            mask = first_ok if r == 0 else band_ok
            o = None
            for half in range(2):
                sink = sinks_ref[2 * m + half]
                kb = kv_scr[j * 2 + half, r * WINDOW:(r + 2) * WINDOW, :]
                vb = kv_scr[4 + j * 2 + half, r * WINDOW:(r + 2) * WINDOW, :]
                s = lax.dot_general(qb, kb, (((1,), (1,)), ((), ())), preferred_element_type=_F32)
                s = jnp.where(mask, s, NEG_INF)
                mx = jnp.maximum(jnp.max(s, axis=-1, keepdims=True), sink)
                p = jnp.exp(s - mx)
                denom = jnp.sum(p, axis=-1, keepdims=True) + jnp.exp(sink - mx)
                oh = _dot(p.astype(_BF16), vb) * (1.0 / denom)
                o = oh if o is None else o + oh
            attn_scr[r * WINDOW:(r + 1) * WINDOW, m * LANES:(m + 1) * LANES] = o.astype(_BF16)

    g2_ref[0] = gate(2).astype(_BF16)

    y_attn = _dot(attn_scr[...], wao_ref[...])
    part_ref[0] = (g0_scr[...] * y_attn + acc_scr[...]).astype(_BF16)

    kv_scr[:, 0:WINDOW, :] = kv_scr[:, tm:tm + WINDOW, :]
    z_scr[0:SUBLANES, :] = z_scr[tm:tm + SUBLANES, :]


def _mixer_in(x, sinks, nw, win, bg, cos_t, sa_t, sb_t, cw, wao, wco, layer, cast_weights):
    b, l, _ = x.shape
    tm = TOKEN_TILE
    nchunk = tm // SSM_CHUNK
    grid = (b, l // tm)
    in_specs = [
        pl.BlockSpec(memory_space=pltpu.SMEM),
        pl.BlockSpec((1, tm, D_MODEL), lambda bi, li: (bi, li, 0)),
        _layer_spec((1, D_MODEL), layer),
        _const_spec((D_MODEL, IN_COLS)),
        _layer_spec((1, GATE_WIDTH), layer),
        pl.BlockSpec((tm, LANES), lambda bi, li: (li, 0)),
        pl.BlockSpec((tm, LANES), lambda bi, li: (li, 0)),
        pl.BlockSpec((tm, LANES), lambda bi, li: (li, 0)),
        _layer_spec((CONV_K, CONV_WIDTH), layer),
        _const_spec((ATTN_WIDTH, D_MODEL)),
        _const_spec((CONV_WIDTH, D_MODEL)),
    ]
    out_specs = [
        pl.BlockSpec((1, tm, D_MODEL), lambda bi, li: (bi, li, 0)),
        pl.BlockSpec((1, tm, D_MODEL), lambda bi, li: (bi, li, 0)),
        _ssm_io_spec(nchunk),
    ]
    out_shape = [
        jax.ShapeDtypeStruct((b, l, D_MODEL), _BF16),
        jax.ShapeDtypeStruct((b, l, D_MODEL), _BF16),
        jax.ShapeDtypeStruct((SSM_SLABS, l // SSM_CHUNK, b, SSM_CHUNK, LANES), _F32),
    ]
    scratch = [
        pltpu.VMEM((8, WINDOW + tm, LANES), _BF16),
        pltpu.VMEM((SUBLANES + tm, CONV_WIDTH), _F32),
        pltpu.VMEM((tm, ATTN_WIDTH), _BF16),
        pltpu.VMEM((tm, CONV_WIDTH), _BF16),
        pltpu.VMEM((tm, D_MODEL), _F32),
        pltpu.VMEM((tm, D_MODEL), _F32),
    ]
    cast_in, cast_out, cast_shape = _cast_plan(cast_weights, layer, grid)
    return pl.pallas_call(
        _with_casts(_mixer_in_kernel, len(in_specs), len(out_specs), len(cast_weights)),
        grid=grid, in_specs=in_specs + cast_in, out_specs=out_specs + cast_out,
        out_shape=out_shape + cast_shape, scratch_shapes=scratch, name="mixer_in",
        compiler_params=pltpu.CompilerParams(
            dimension_semantics=("arbitrary", "arbitrary"), vmem_limit_bytes=VMEM_LIMIT),
    )(sinks, x, nw, win, bg, cos_t, sa_t, sb_t, cw, wao, wco, *cast_weights)


def _ssm_kernel(u_ref, wc_ref, cct_ref, kc_ref, ar_ref, ai_ref, y_ref,
                mb_scr, mt_scr, mct_scr, ucat_scr, s_scr, xr_scr, xi_scr, *, batch):
    t = SSM_CHUNK
    rows = u_ref.shape[1] // t

    @pl.when(pl.program_id(1) == 0)
    def _():
        chan_bits, state_bits = SSM_GROUP.bit_length() - 1, SSM_STATE.bit_length() - 1
        r_grp = lax.broadcasted_iota(jnp.int32, (LANES, 2 * SLAB_STATE), 0) >> chan_bits
        c_grp = (lax.broadcasted_iota(jnp.int32, (LANES, 2 * SLAB_STATE), 1) & (SLAB_STATE - 1)) >> state_bits
        same_state = r_grp == c_grp

        def expand(tbl):
            return jnp.where(same_state, jnp.tile(tbl, (SLAB_GROUPS, 1)), 0.0)

        r_grp = lax.broadcasted_iota(jnp.int32, (LANES, LANES), 0) >> chan_bits
        c_grp = lax.broadcasted_iota(jnp.int32, (LANES, LANES), 1) >> chan_bits
        same_chan = r_grp == c_grp
        zero_blk = jnp.zeros((LANES, LANES), _BF16)
        blks = []
        for st in range(t):
            rs = slice(st * LANES, (st + 1) * LANES)
            mb_scr[rs, :] = expand(wc_ref[0, st]).astype(_BF16)
            mct_scr[rs, :] = expand(cct_ref[0, st]).astype(_BF16)
            blks.append(jnp.where(same_chan, jnp.tile(kc_ref[0, st], (SLAB_GROUPS, 1)), 0.0).astype(_BF16))
        for s_in in range(t):
            for s_out in range(t):
                mt_scr[s_in * LANES:(s_in + 1) * LANES, s_out * LANES:(s_out + 1) * LANES] = (
                    blks[s_out - s_in] if s_out >= s_in else zero_blk)
        xr_scr[...] = jnp.zeros_like(xr_scr)
        xi_scr[...] = jnp.zeros_like(xi_scr)

    for st in range(t):
        ucat_scr[:, st * LANES:(st + 1) * LANES] = u_ref[0, pl.ds(st, rows, stride=t), :].astype(_BF16)

    s_scr[...] = _dot(ucat_scr[...], mb_scr[...])

    ar = jnp.broadcast_to(ar_ref[0], (batch, SLAB_STATE))
    ai = jnp.broadcast_to(ai_ref[0], (batch, SLAB_STATE))

    def step(c, carry):
        xr, xi = carry
        r0 = pl.multiple_of(c * batch, batch)
        sr = s_scr[pl.ds(r0, batch), 0:SLAB_STATE]
        si = s_scr[pl.ds(r0, batch), SLAB_STATE:2 * SLAB_STATE]
        s_scr[pl.ds(r0, batch), 0:SLAB_STATE] = xr
        s_scr[pl.ds(r0, batch), SLAB_STATE:2 * SLAB_STATE] = xi
        return ar * xr - ai * xi + sr, ar * xi + ai * xr + si

    xr, xi = lax.fori_loop(0, rows // batch, step, (xr_scr[...], xi_scr[...]))
    xr_scr[...] = xr
    xi_scr[...] = xi

    pair = 2 * LANES
    toeplitz = jnp.concatenate(
        [_dot(ucat_scr[:, 0:c0 + pair], mt_scr[0:c0 + pair, c0:c0 + pair]) for c0 in range(0, t * LANES, pair)],
        axis=1)
    y = toeplitz + lax.dot_general(
        s_scr[...].astype(_BF16), mct_scr[...], (((1,), (1,)), ((), ())), preferred_element_type=_F32)
    for st in range(t):
        y_ref[0, pl.ds(st, rows, stride=t), :] = y[:, st * LANES:(st + 1) * LANES]


def _ssm(u, wc, cct, kc, ar, ai, layer, batch):
    nslab, tok_rows, _ = u.shape
    blk = SSM_CHUNK_BLOCK * batch * SSM_CHUNK
    rows = SSM_CHUNK_BLOCK * batch
    width = SSM_CHUNK * LANES
    io = pl.BlockSpec((1, blk, LANES), lambda q, ci: (q, ci, 0))
    par = lambda *shape: pl.BlockSpec((None, 1) + shape, lambda q, ci: (layer, q) + (0,) * len(shape))
    return pl.pallas_call(
        functools.partial(_ssm_kernel, batch=batch),
        grid=(nslab, tok_rows // blk),
        in_specs=[io, par(SSM_CHUNK, SSM_GROUP, 2 * SLAB_STATE), par(SSM_CHUNK, SSM_GROUP, 2 * SLAB_STATE),
                  par(SSM_CHUNK, SSM_GROUP, LANES), par(1, SLAB_STATE), par(1, SLAB_STATE)],
        out_specs=io,
        out_shape=jax.ShapeDtypeStruct(u.shape, _F32),
        scratch_shapes=[
            pltpu.VMEM((width, 2 * SLAB_STATE), _BF16),
            pltpu.VMEM((width, width), _BF16),
            pltpu.VMEM((width, 2 * SLAB_STATE), _BF16),
            pltpu.VMEM((rows, width), _BF16),
            pltpu.VMEM((rows, 2 * SLAB_STATE), _F32),
            pltpu.VMEM((batch, SLAB_STATE), _F32),
            pltpu.VMEM((batch, SLAB_STATE), _F32),
        ],
        name="ssm",
        compiler_params=pltpu.CompilerParams(
            dimension_semantics=("arbitrary", "arbitrary"), vmem_limit_bytes=VMEM_LIMIT),
    )(u, wc, cct, kc, ar, ai)


def _ssm_operators(a_re, a_im, b_re, b_im, c_re, c_im, d, log_dt):
    t = SSM_CHUNK
    p, hh, nq, ng = SSM_STATE, SSM_GROUP, SSM_SLABS, SLAB_GROUPS
    depth = a_re.shape[0]
    hi = lax.Precision.HIGHEST
    a_re, a_im = a_re.astype(_F32), a_im.astype(_F32)
    dt = jnp.exp(log_dt.astype(_F32))[:, :, None]
    b_re, b_im = b_re.astype(_F32), b_im.astype(_F32)
    c_re, c_im = c_re.astype(_F32), c_im.astype(_F32)

    def powers(ar_, ai_, dt_, n):
        mag, ang = jnp.exp(ar_ * dt_ * n), ai_ * dt_ * n
        return mag * jnp.cos(ang), mag * jnp.sin(ang)

    def zoh(ar_, ai_, dt_):
        lr, li_ = powers(ar_, ai_, dt_, 1.0)
        nr, den = lr - 1.0, ar_ * ar_ + ai_ * ai_
        return (nr * ar_ + li_ * ai_) / den, (li_ * ar_ - nr * ai_) / den

    steps = jnp.arange(t, dtype=_F32)[:, None, None]
    pr, pi = powers(a_re[:, :, None, None], a_im[:, :, None, None], dt[:, :, None, None], steps)
    cr, ci = c_re[:, :, None], c_im[:, :, None]
    lhs = jnp.concatenate([cr * pr - ci * pi, -(cr * pi + ci * pr)], axis=-1)
    f_re, f_im = zoh(a_re, a_im, dt)
    rhs = jnp.concatenate([f_re[..., None] * b_re - f_im[..., None] * b_im,
                           f_re[..., None] * b_im + f_im[..., None] * b_re], axis=-2)
    kern = jnp.einsum("ygmc,ygck->ygmk", lhs.reshape(depth, -1, t * hh, 2 * p), rhs, precision=hi)
    kern = kern.reshape(depth, -1, t, hh, hh)
    skip = jnp.eye(hh, dtype=_F32) * d.astype(_F32).reshape(depth, -1, hh, 1)
    kern = jnp.concatenate([kern[:, :, :1] + skip[:, :, None], kern[:, :, 1:]], axis=2)
    kc = kern.reshape(depth, nq, ng, t, hh, hh).transpose(0, 1, 3, 5, 2, 4).reshape(depth, nq, t, hh, LANES)

    lane = lambda v: v.reshape(depth, nq, 1, ng * p)
    la_re, la_im = lane(a_re), lane(a_im)
    ldt = lane(jnp.broadcast_to(dt, a_re.shape))
    lf_re, lf_im = zoh(la_re, la_im, ldt)
    to_rows = lambda v, axes: v.reshape(depth, nq, ng, *v.shape[2:]).transpose(axes).reshape(depth, nq, hh, ng * p)
    lb_re, lb_im = to_rows(b_re, (0, 1, 4, 2, 3)), to_rows(b_im, (0, 1, 4, 2, 3))
    lc_re, lc_im = to_rows(c_re, (0, 1, 3, 2, 4)), to_rows(c_im, (0, 1, 3, 2, 4))
    bb_re, bb_im = lf_re * lb_re - lf_im * lb_im, lf_re * lb_im + lf_im * lb_re

    rev = (t - 1 - jnp.arange(t, dtype=_F32))[:, None, None]
    rr, ri = powers(la_re[:, :, None], la_im[:, :, None], ldt[:, :, None], rev)
    wc = jnp.concatenate([rr * bb_re[:, :, None] - ri * bb_im[:, :, None],
                          rr * bb_im[:, :, None] + ri * bb_re[:, :, None]], axis=-1)
    fwd = (1.0 + jnp.arange(t, dtype=_F32))[:, None, None]
    fr, fi = powers(la_re[:, :, None], la_im[:, :, None], ldt[:, :, None], fwd)
    cct = jnp.concatenate([lc_re[:, :, None] * fr - lc_im[:, :, None] * fi,
                           -(lc_re[:, :, None] * fi + lc_im[:, :, None] * fr)], axis=-1)
    ar, ai = powers(la_re, la_im, ldt, float(t))
    return wc, cct, kc, ar, ai


def _mixer_out_kernel(x_ref, part_ref, g2_ref, y_ref, wglu_ref, wso_ref, wmix_ref, nf_ref,
                      wfi_ref, wfo_ref, nlast_ref, out_ref, *, final):
    tm = x_ref.shape[1]
    y = jnp.concatenate([y_ref[sl].reshape(tm, LANES) for sl in range(SSM_SLABS)], axis=1)
    ys = 0.5 * y * (1.0 + jnp.tanh(math.sqrt(2.0 / math.pi) * (y + 0.044715 * (y * y * y))))
    glu = ys * _sigmoid(_dot(ys.astype(_BF16), wglu_ref[...]))
    y_ssm = _dot(glu.astype(_BF16), wso_ref[...])
    merged = part_ref[0].astype(_F32) + g2_ref[0].astype(_F32) * y_ssm
    x1 = x_ref[0] + _dot(merged.astype(_BF16), wmix_ref[...])

    h = _rmsnorm(x1, nf_ref[...]).astype(_BF16)
    gt = _dot(h, wfi_ref[:, 0:FFN_HIDDEN])
    up = _dot(h, wfi_ref[:, FFN_HIDDEN:2 * FFN_HIDDEN])
    act = (gt * _sigmoid(gt) * up).astype(_BF16)
    x2 = x1 + _dot(act, wfo_ref[...])
    if final:
        x2 = _rmsnorm(x2, nlast_ref[...])
    out_ref[0] = x2


def _mixer_out(x, part, g2, y, wglu, wso, wmix, nf, wfi, wfo, nlast, layer, final, cast_weights):
    b, l, _ = x.shape
    tm = TOKEN_TILE
    nchunk = tm // SSM_CHUNK
    grid = (b, l // tm)
    tok = pl.BlockSpec((1, tm, D_MODEL), lambda bi, li: (bi, li, 0))
    in_specs = [
        tok, tok, tok,
        _ssm_io_spec(nchunk),
        _const_spec((SSM_WIDTH, SSM_WIDTH)),
        _const_spec((SSM_WIDTH, D_MODEL)),
        _const_spec((D_MODEL, D_MODEL)),
        _layer_spec((1, D_MODEL), layer),
        _const_spec((D_MODEL, 2 * FFN_HIDDEN)),
        _const_spec((FFN_HIDDEN, D_MODEL)),
        _const_spec((1, D_MODEL)),
    ]
    cast_in, cast_out, cast_shape = _cast_plan(cast_weights, layer + 1, grid)
    body = functools.partial(_mixer_out_kernel, final=final)
    return pl.pallas_call(
        _with_casts(body, len(in_specs), 1, len(cast_weights)),
        grid=grid, in_specs=in_specs + cast_in, out_specs=[tok] + cast_out,
        out_shape=[jax.ShapeDtypeStruct((b, l, D_MODEL), _F32)] + cast_shape,
        name="mixer_out",
        compiler_params=pltpu.CompilerParams(
            dimension_semantics=("arbitrary", "arbitrary"), vmem_limit_bytes=VMEM_LIMIT),
    )(x, part, g2, y, wglu, wso, wmix, nf, wfi, wfo, nlast, *cast_weights)


def _rope_tables(seq_len):
    half = ROT_DIM // 2
    pos = jnp.arange(seq_len, dtype=_F32)
    inv_freq = ROPE_THETA ** (-jnp.arange(0, ROT_DIM, 2, dtype=_F32) / ROT_DIM)
    ang = pos[:, None] * inv_freq[None, :]
    cos, sin = jnp.cos(ang), jnp.sin(ang)
    ones = jnp.ones((seq_len, HEAD_DIM - ROT_DIM), _F32)
    zeros_h = jnp.zeros((seq_len, half), _F32)
    zeros_r = jnp.zeros((seq_len, HEAD_DIM - ROT_DIM), _F32)
    cos_h = jnp.concatenate([cos, cos, ones], axis=1)
    sa_h = jnp.concatenate([-sin, zeros_h, zeros_r], axis=1)
    sb_h = jnp.concatenate([zeros_h, sin, zeros_r], axis=1)
    rep = LANES // HEAD_DIM
    return jnp.tile(cos_h, (1, rep)), jnp.tile(sa_h, (1, rep)), jnp.tile(sb_h, (1, rep))


def kernel(x, norm_mix, w_in, b_gate, attn_sinks, w_attn_o, conv_w, w_conv_o, ssm_a_re, ssm_a_im,
           ssm_b_re, ssm_b_im, ssm_c_re, ssm_c_im, ssm_d, ssm_log_dt, w_ssm_glu, w_ssm_o, w_mix_o,
           norm_ffn, w_ffn_in, w_ffn_out, norm_final):
    b, l, d_model = x.shape
    depth = w_in.shape[0]
    assert d_model == D_MODEL and l % TOKEN_TILE == 0 and b % SUBLANES == 0
    assert l % (SSM_CHUNK * SSM_CHUNK_BLOCK) == 0
    assert w_in.shape[2] == IN_COLS and w_ffn_out.shape[1] == FFN_HIDDEN
    cos_t, sa_t, sb_t = _rope_tables(l)
    f32 = lambda p: p.astype(_F32)
    norm_mix, norm_ffn = f32(norm_mix)[:, None, :], f32(norm_ffn)[:, None, :]
    b_gate, conv_w, attn_sinks = f32(b_gate)[:, None, :], f32(conv_w), f32(attn_sinks)
    nlast = f32(norm_final).reshape(1, D_MODEL)
    wc, cct, kc, ar, ai = _ssm_operators(
        ssm_a_re, ssm_a_im, ssm_b_re, ssm_b_im, ssm_c_re, ssm_c_im, ssm_d, ssm_log_dt)
    in_weights = tuple(f32(w) for w in (w_in, w_attn_o, w_conv_o))
    out_weights = tuple(f32(w) for w in (w_ssm_glu, w_ssm_o, w_mix_o, w_ffn_in, w_ffn_out))
    w_mixer_in = [w[0].astype(_BF16) for w in in_weights]
    x = f32(x)
    for i in range(depth):
        last = i == depth - 1
        win, wao, wco = w_mixer_in
        part, g2, u, *w_mixer_out = _mixer_in(
            x, attn_sinks[i], norm_mix, win, b_gate, cos_t, sa_t, sb_t, conv_w, wao, wco, i, out_weights)
        y = _ssm(u.reshape(SSM_SLABS, l * b, LANES), wc, cct, kc, ar, ai, i, b).reshape(u.shape)
        x, *w_mixer_in = _mixer_out(
            x, part, g2, y, *w_mixer_out[:3], norm_ffn, *w_mixer_out[3:], nlast, i, last,
            () if last else in_weights)
    return x
```

```python
import functools
import math

import jax
import jax.numpy as jnp
from jax import lax
from jax.experimental import pallas as pl
from jax.experimental.pallas import tpu as pltpu

D_MODEL = 1024
N_Q_HEADS = 8
N_KV_HEADS = 2
HEAD_DIM = 64
Q_GROUP = N_Q_HEADS // N_KV_HEADS
WINDOW = 128
ROPE_THETA = 500000.0
ROT_DIM = HEAD_DIM // 4
ATTN_WIDTH = N_Q_HEADS * HEAD_DIM
KV_WIDTH = N_KV_HEADS * HEAD_DIM
NEG_INF = -1e30
CONV_WIDTH = D_MODEL // 2
CONV_K = 3
SSM_WIDTH = D_MODEL // 2
SSM_GROUP = 16
SSM_GROUPS = SSM_WIDTH // SSM_GROUP
SSM_STATE = 64
GATE_WIDTH = 3 * D_MODEL
FFN_HIDDEN = 2816
NORM_EPS = 1e-6

_Q0 = 0
_KV0 = _Q0 + ATTN_WIDTH
_CB0 = _KV0 + 2 * KV_WIDTH
_CC0 = _CB0 + CONV_WIDTH
_CX0 = _CC0 + CONV_WIDTH
_U0 = _CX0 + CONV_WIDTH
_G0 = _U0 + SSM_WIDTH
IN_COLS = _G0 + GATE_WIDTH

LANES = 128
SUBLANES = 8
BF16_ROWS = 2 * SUBLANES
SSM_CHUNK = 8
SSM_SLABS = SSM_WIDTH // LANES
SLAB_GROUPS = LANES // SSM_GROUP
SLAB_STATE = SLAB_GROUPS * SSM_STATE
TOKEN_TILE = 512
SSM_CHUNK_BLOCK = 128
MATMUL_ROWS = 256
VMEM_LIMIT = 56 * 1024 * 1024

_BF16 = jnp.bfloat16
_F32 = jnp.float32


def _dot(a, b):
    m = a.shape[0]
    if m <= MATMUL_ROWS:
        return jnp.dot(a, b, preferred_element_type=_F32)
    return jnp.concatenate(
        [jnp.dot(a[i:i + MATMUL_ROWS], b, preferred_element_type=_F32) for i in range(0, m, MATMUL_ROWS)], axis=0)


def _sigmoid(x):
    return 0.5 * jnp.tanh(0.5 * x) + 0.5


def _rmsnorm(x, w):
    ms = jnp.mean(x * x, axis=-1, keepdims=True)
    return x * lax.rsqrt(ms + NORM_EPS) * w


def _const_spec(shape):
    return pl.BlockSpec(shape, lambda *_: (0,) * len(shape), pipeline_mode=pl.Buffered(1))


def _layer_spec(shape, layer):
    return pl.BlockSpec((None,) + shape, lambda *_: (layer,) + (0,) * len(shape), pipeline_mode=pl.Buffered(1))


def _cast_plan(weights, layer, grid):
    nsteps = grid[0] * grid[1]
    in_specs, out_specs, out_shapes = [], [], []
    for w in weights:
        _, rows, cols = w.shape
        ncol = next(n for n in (1, 2, 4, 8)
                    if (rows * n) % (nsteps * BF16_ROWS) == 0 and cols % (n * LANES) == 0)
        nrow = nsteps // ncol
        block = (rows // nrow, cols // ncol)

        def index(bi, li, nrow=nrow):
            step = bi * grid[1] + li
            return step % nrow, step // nrow

        in_specs.append(pl.BlockSpec((None,) + block, lambda bi, li, index=index: (layer,) + index(bi, li)))
        out_specs.append(pl.BlockSpec(block, index))
        out_shapes.append(jax.ShapeDtypeStruct((rows, cols), _BF16))
    return in_specs, out_specs, out_shapes


def _with_casts(body, n_in, n_out, n_cast):
    def kernel(*refs):
        ins, rest = refs[:n_in], refs[n_in:]
        cast_src, rest = rest[:n_cast], rest[n_cast:]
        outs, rest = rest[:n_out], rest[n_out:]
        cast_dst, scratch = rest[:n_cast], rest[n_cast:]
        body(*ins, *outs, *scratch)
        for src, dst in zip(cast_src, cast_dst):
            dst[...] = src[...].astype(_BF16)
    return kernel


def _ssm_io_spec(nchunk):
    return pl.BlockSpec((SSM_SLABS, nchunk, None, SSM_CHUNK, LANES), lambda bi, li: (0, li, bi, 0, 0))


def _mixer_in_kernel(sinks_ref, x_ref, nw_ref, win_ref, bg_ref, cos_ref, sa_ref, sb_ref, cw_ref,
                     wao_ref, wco_ref, part_ref, g2_ref, u_ref,
                     kv_scr, z_scr, attn_scr, yc_scr, acc_scr, g0_scr):
    tm = x_ref.shape[1]
    li = pl.program_id(1)

    @pl.when(li == 0)
    def _():
        kv_scr[:, 0:WINDOW, :] = jnp.zeros((8, WINDOW, LANES), _BF16)
        z_scr[0:SUBLANES, :] = jnp.zeros((SUBLANES, CONV_WIDTH), _F32)

    x = x_ref[0]
    h = _rmsnorm(x, nw_ref[...]).astype(_BF16)

    cos = cos_ref[...]
    sa = sa_ref[...]
    sb = sb_ref[...]

    def rope(t):
        return (t * cos + pltpu.roll(t, LANES - ROT_DIM // 2, axis=1) * sa
                + pltpu.roll(t, ROT_DIM // 2, axis=1) * sb)

    kv = _dot(h, win_ref[:, _KV0:_KV0 + 2 * KV_WIDTH])
    lane = lax.broadcasted_iota(jnp.int32, (tm, LANES), 1)
    lo = lane < HEAD_DIM
    for kind in range(2):
        t = kv[:, kind * LANES:(kind + 1) * LANES]
        if kind == 0:
            t = rope(t)
        t_sw = pltpu.roll(t, HEAD_DIM, axis=1)
        zero = jnp.zeros_like(t)
        variants = (jnp.where(lo, t, zero), jnp.where(lo, zero, t_sw),
                    jnp.where(lo, t_sw, zero), jnp.where(lo, zero, t))
        for n, val in enumerate(variants):
            kv_scr[kind * 4 + n, WINDOW:WINDOW + tm, :] = val.astype(_BF16)

    q = _dot(h, win_ref[:, _Q0:_Q0 + ATTN_WIDTH])
    row = lax.broadcasted_iota(jnp.int32, (WINDOW, 2 * WINDOW), 0)
    col = lax.broadcasted_iota(jnp.int32, (WINDOW, 2 * WINDOW), 1)
    band_ok = (col > row) & (col <= row + WINDOW)
    first_ok = band_ok & (col >= jnp.where(li > 0, 0, WINDOW))

    def gate(n):
        c0 = _G0 + n * D_MODEL
        return _sigmoid(_dot(h, win_ref[:, c0:c0 + D_MODEL]) + bg_ref[:, n * D_MODEL:(n + 1) * D_MODEL])


    cb = _dot(h, win_ref[:, _CB0:_CB0 + CONV_WIDTH])
    cc = _dot(h, win_ref[:, _CC0:_CC0 + CONV_WIDTH])
    cx = _dot(h, win_ref[:, _CX0:_CX0 + CONV_WIDTH])
    z = cc * cx
    z_scr[SUBLANES:SUBLANES + tm, :] = z
    z1 = z_scr[SUBLANES - 1:SUBLANES - 1 + tm, :]
    z2 = z_scr[SUBLANES - 2:SUBLANES - 2 + tm, :]
    conv = cw_ref[0:1, :] * z2 + cw_ref[1:2, :] * z1 + cw_ref[2:3, :] * z
    yc_scr[...] = (cb * conv).astype(_BF16)
    acc_scr[...] = gate(1) * _dot(yc_scr[...], wco_ref[...])

    g0_scr[...] = gate(0)
    u = _dot(h, win_ref[:, _U0:_U0 + SSM_WIDTH])
    nchunk = tm // SSM_CHUNK
    for sl in range(SSM_SLABS):
        u_ref[sl] = u[:, sl * LANES:(sl + 1) * LANES].reshape(nchunk, SSM_CHUNK, LANES)

    for m in range(N_Q_HEADS // 2):
        qp = (rope(q[:, m * LANES:(m + 1) * LANES]) * (HEAD_DIM ** -0.5)).astype(_BF16)
        j = (2 * m) // Q_GROUP
        for r in range(tm // WINDOW):
            qb = qp[r * WINDOW:(r + 1) * WINDOW]
            mask = first_ok if r == 0 else band_ok
            o = None
            for half in range(2):
                sink = sinks_ref[2 * m + half]
                kb = kv_scr[j * 2 + half, r * WINDOW:(r + 2) * WINDOW, :]
                vb = kv_scr[4 + j * 2 + half, r * WINDOW:(r + 2) * WINDOW, :]
                s = lax.dot_general(qb, kb, (((1,), (1,)), ((), ())), preferred_element_type=_F32)
                s = jnp.where(mask, s, NEG_INF)
                mx = jnp.maximum(jnp.max(s, axis=-1, keepdims=True), sink)
                p = jnp.exp(s - mx)
                denom = jnp.sum(p, axis=-1, keepdims=True) + jnp.exp(sink - mx)
                oh = _dot(p.astype(_BF16), vb) * (1.0 / denom)
                o = oh if o is None else o + oh
            attn_scr[r * WINDOW:(r + 1) * WINDOW, m * LANES:(m + 1) * LANES] = o.astype(_BF16)

    g2_ref[0] = gate(2).astype(_BF16)

    y_attn = _dot(attn_scr[...], wao_ref[...])
    part_ref[0] = (g0_scr[...] * y_attn + acc_scr[...]).astype(_BF16)

    kv_scr[:, 0:WINDOW, :] = kv_scr[:, tm:tm + WINDOW, :]
    z_scr[0:SUBLANES, :] = z_scr[tm:tm + SUBLANES, :]


def _mixer_in(x, sinks, nw, win, bg, cos_t, sa_t, sb_t, cw, wao, wco, layer, cast_weights):
    b, l, _ = x.shape
    tm = TOKEN_TILE
    nchunk = tm // SSM_CHUNK
    grid = (b, l // tm)
    in_specs = [
        pl.BlockSpec(memory_space=pltpu.SMEM),
        pl.BlockSpec((1, tm, D_MODEL), lambda bi, li: (bi, li, 0)),
        _layer_spec((1, D_MODEL), layer),
        _const_spec((D_MODEL, IN_COLS)),
        _layer_spec((1, GATE_WIDTH), layer),
        pl.BlockSpec((tm, LANES), lambda bi, li: (li, 0)),
        pl.BlockSpec((tm, LANES), lambda bi, li: (li, 0)),
        pl.BlockSpec((tm, LANES), lambda bi, li: (li, 0)),
        _layer_spec((CONV_K, CONV_WIDTH), layer),
        _const_spec((ATTN_WIDTH, D_MODEL)),
        _const_spec((CONV_WIDTH, D_MODEL)),
    ]
    out_specs = [
        pl.BlockSpec((1, tm, D_MODEL), lambda bi, li: (bi, li, 0)),
        pl.BlockSpec((1, tm, D_MODEL), lambda bi, li: (bi, li, 0)),
        _ssm_io_spec(nchunk),
    ]
    out_shape = [
        jax.ShapeDtypeStruct((b, l, D_MODEL), _BF16),
        jax.ShapeDtypeStruct((b, l, D_MODEL), _BF16),
        jax.ShapeDtypeStruct((SSM_SLABS, l // SSM_CHUNK, b, SSM_CHUNK, LANES), _F32),
    ]
    scratch = [
        pltpu.VMEM((8, WINDOW + tm, LANES), _BF16),
        pltpu.VMEM((SUBLANES + tm, CONV_WIDTH), _F32),
        pltpu.VMEM((tm, ATTN_WIDTH), _BF16),
        pltpu.VMEM((tm, CONV_WIDTH), _BF16),
        pltpu.VMEM((tm, D_MODEL), _F32),
        pltpu.VMEM((tm, D_MODEL), _F32),
    ]
    cast_in, cast_out, cast_shape = _cast_plan(cast_weights, layer, grid)
    return pl.pallas_call(
        _with_casts(_mixer_in_kernel, len(in_specs), len(out_specs), len(cast_weights)),
        grid=grid, in_specs=in_specs + cast_in, out_specs=out_specs + cast_out,
        out_shape=out_shape + cast_shape, scratch_shapes=scratch, name="mixer_in",
        compiler_params=pltpu.CompilerParams(
            dimension_semantics=("arbitrary", "arbitrary"), vmem_limit_bytes=VMEM_LIMIT),
    )(sinks, x, nw, win, bg, cos_t, sa_t, sb_t, cw, wao, wco, *cast_weights)


def _ssm_kernel(u_ref, wc_ref, cct_ref, kc_ref, ar_ref, ai_ref, y_ref,
                mb_scr, mt_scr, mct_scr, ucat_scr, s_scr, xr_scr, xi_scr, *, batch):
    t = SSM_CHUNK
    rows = u_ref.shape[1] // t

    @pl.when(pl.program_id(1) == 0)
    def _():
        chan_bits, state_bits = SSM_GROUP.bit_length() - 1, SSM_STATE.bit_length() - 1
        r_grp = lax.broadcasted_iota(jnp.int32, (LANES, 2 * SLAB_STATE), 0) >> chan_bits
        c_grp = (lax.broadcasted_iota(jnp.int32, (LANES, 2 * SLAB_STATE), 1) & (SLAB_STATE - 1)) >> state_bits
        same_state = r_grp == c_grp

        def expand(tbl):
            return jnp.where(same_state, jnp.tile(tbl, (SLAB_GROUPS, 1)), jnp.zeros((), tbl.dtype))

        r_grp = lax.broadcasted_iota(jnp.int32, (LANES, LANES), 0) >> chan_bits
        c_grp = lax.broadcasted_iota(jnp.int32, (LANES, LANES), 1) >> chan_bits
        same_chan = r_grp == c_grp
        zero_blk = jnp.zeros((LANES, LANES), _BF16)
        blks = []
        for st in range(t):
            rs = slice(st * LANES, (st + 1) * LANES)
            mb_scr[rs, :] = expand(wc_ref[0, st]).astype(_BF16)
            mct_scr[rs, :] = expand(cct_ref[0, st]).astype(_BF16)
            blks.append(jnp.where(same_chan, jnp.tile(kc_ref[0, st], (SLAB_GROUPS, 1)), 0.0).astype(_BF16))
        for s_in in range(t):
            for s_out in range(t):
                mt_scr[s_in * LANES:(s_in + 1) * LANES, s_out * LANES:(s_out + 1) * LANES] = (
                    blks[s_out - s_in] if s_out >= s_in else zero_blk)
        xr_scr[...] = jnp.zeros_like(xr_scr)
        xi_scr[...] = jnp.zeros_like(xi_scr)

    for st in range(t):
        ucat_scr[:, st * LANES:(st + 1) * LANES] = u_ref[0, pl.ds(st, rows, stride=t), :].astype(_BF16)

    s_scr[...] = _dot(ucat_scr[...], mb_scr[...])

    ar = jnp.broadcast_to(ar_ref[0], (batch, SLAB_STATE))
    ai = jnp.broadcast_to(ai_ref[0], (batch, SLAB_STATE))

    xr, xi = xr_scr[...], xi_scr[...]
    for c in range(rows // batch):
        rs = slice(c * batch, (c + 1) * batch)
        sr = s_scr[rs, 0:SLAB_STATE]
        si = s_scr[rs, SLAB_STATE:2 * SLAB_STATE]
        s_scr[rs, 0:SLAB_STATE] = xr
        s_scr[rs, SLAB_STATE:2 * SLAB_STATE] = xi
        xr, xi = ar * xr - ai * xi + sr, ar * xi + ai * xr + si
    xr_scr[...] = xr
    xi_scr[...] = xi

    pair = 2 * LANES
    toeplitz = jnp.concatenate(
        [_dot(ucat_scr[:, 0:c0 + pair], mt_scr[0:c0 + pair, c0:c0 + pair]) for c0 in range(0, t * LANES, pair)],
        axis=1)
    y = toeplitz + lax.dot_general(
        s_scr[...].astype(_BF16), mct_scr[...], (((1,), (1,)), ((), ())), preferred_element_type=_F32)
    for st in range(t):
        y_ref[0, pl.ds(st, rows, stride=t), :] = y[:, st * LANES:(st + 1) * LANES]


def _ssm(u, wc, cct, kc, ar, ai, layer, batch):
    nslab, tok_rows, _ = u.shape
    blk = SSM_CHUNK_BLOCK * batch * SSM_CHUNK
    rows = SSM_CHUNK_BLOCK * batch
    width = SSM_CHUNK * LANES
    io = pl.BlockSpec((1, blk, LANES), lambda q, ci: (q, ci, 0))
    par = lambda *shape: pl.BlockSpec((None, 1) + shape, lambda q, ci: (layer, q) + (0,) * len(shape))
    return pl.pallas_call(
        functools.partial(_ssm_kernel, batch=batch),
        grid=(nslab, tok_rows // blk),
        in_specs=[io, par(SSM_CHUNK, SSM_GROUP, 2 * SLAB_STATE), par(SSM_CHUNK, SSM_GROUP, 2 * SLAB_STATE),
                  par(SSM_CHUNK, SSM_GROUP, LANES), par(1, SLAB_STATE), par(1, SLAB_STATE)],
        out_specs=io,
        out_shape=jax.ShapeDtypeStruct(u.shape, _F32),
        scratch_shapes=[
            pltpu.VMEM((width, 2 * SLAB_STATE), _BF16),
            pltpu.VMEM((width, width), _BF16),
            pltpu.VMEM((width, 2 * SLAB_STATE), _BF16),
            pltpu.VMEM((rows, width), _BF16),
            pltpu.VMEM((rows, 2 * SLAB_STATE), _F32),
            pltpu.VMEM((batch, SLAB_STATE), _F32),
            pltpu.VMEM((batch, SLAB_STATE), _F32),
        ],
        name="ssm",
        compiler_params=pltpu.CompilerParams(
            dimension_semantics=("arbitrary", "arbitrary"), vmem_limit_bytes=VMEM_LIMIT),
    )(u, wc, cct, kc, ar, ai)


def _ssm_operators(a_re, a_im, b_re, b_im, c_re, c_im, d, log_dt):
    t = SSM_CHUNK
    p, hh, nq, ng = SSM_STATE, SSM_GROUP, SSM_SLABS, SLAB_GROUPS
    depth = a_re.shape[0]
    hi = lax.Precision.HIGHEST
    a_re, a_im = a_re.astype(_F32), a_im.astype(_F32)
    dt = jnp.exp(log_dt.astype(_F32))[:, :, None]
    b_re, b_im = b_re.astype(_F32), b_im.astype(_F32)
    c_re, c_im = c_re.astype(_F32), c_im.astype(_F32)

    def powers(ar_, ai_, dt_, n):
        mag, ang = jnp.exp(ar_ * dt_ * n), ai_ * dt_ * n
        return mag * jnp.cos(ang), mag * jnp.sin(ang)

    def zoh(ar_, ai_, dt_):
        lr, li_ = powers(ar_, ai_, dt_, 1.0)
        nr, den = lr - 1.0, ar_ * ar_ + ai_ * ai_
        return (nr * ar_ + li_ * ai_) / den, (li_ * ar_ - nr * ai_) / den

    steps = jnp.arange(t, dtype=_F32)[:, None, None]
    pr, pi = powers(a_re[:, :, None, None], a_im[:, :, None, None], dt[:, :, None, None], steps)
    cr, ci = c_re[:, :, None], c_im[:, :, None]
    lhs = jnp.concatenate([cr * pr - ci * pi, -(cr * pi + ci * pr)], axis=-1)
    f_re, f_im = zoh(a_re, a_im, dt)
    rhs = jnp.concatenate([f_re[..., None] * b_re - f_im[..., None] * b_im,
                           f_re[..., None] * b_im + f_im[..., None] * b_re], axis=-2)
    kern = jnp.einsum("ygmc,ygck->ygmk", lhs.reshape(depth, -1, t * hh, 2 * p), rhs, precision=hi)
    kern = kern.reshape(depth, -1, t, hh, hh)
    skip = jnp.eye(hh, dtype=_F32) * d.astype(_F32).reshape(depth, -1, hh, 1)
    kern = jnp.concatenate([kern[:, :, :1] + skip[:, :, None], kern[:, :, 1:]], axis=2)
    kc = kern.reshape(depth, nq, ng, t, hh, hh).transpose(0, 1, 3, 5, 2, 4).reshape(depth, nq, t, hh, LANES)

    lane = lambda v: v.reshape(depth, nq, 1, ng * p)
    la_re, la_im = lane(a_re), lane(a_im)
    ldt = lane(jnp.broadcast_to(dt, a_re.shape))
    lf_re, lf_im = zoh(la_re, la_im, ldt)
    to_rows = lambda v, axes: v.reshape(depth, nq, ng, *v.shape[2:]).transpose(axes).reshape(depth, nq, hh, ng * p)
    lb_re, lb_im = to_rows(b_re, (0, 1, 4, 2, 3)), to_rows(b_im, (0, 1, 4, 2, 3))
    lc_re, lc_im = to_rows(c_re, (0, 1, 3, 2, 4)), to_rows(c_im, (0, 1, 3, 2, 4))
    bb_re, bb_im = lf_re * lb_re - lf_im * lb_im, lf_re * lb_im + lf_im * lb_re

    rev = (t - 1 - jnp.arange(t, dtype=_F32))[:, None, None]
    rr, ri = powers(la_re[:, :, None], la_im[:, :, None], ldt[:, :, None], rev)
    wc = jnp.concatenate([rr * bb_re[:, :, None] - ri * bb_im[:, :, None],
                          rr * bb_im[:, :, None] + ri * bb_re[:, :, None]], axis=-1)
    fwd = (1.0 + jnp.arange(t, dtype=_F32))[:, None, None]
    fr, fi = powers(la_re[:, :, None], la_im[:, :, None], ldt[:, :, None], fwd)
    cct = jnp.concatenate([lc_re[:, :, None] * fr - lc_im[:, :, None] * fi,
                           -(lc_re[:, :, None] * fi + lc_im[:, :, None] * fr)], axis=-1)
    ar, ai = powers(la_re, la_im, ldt, float(t))
    return wc.astype(_BF16), cct.astype(_BF16), kc, ar, ai


def _mixer_out_kernel(x_ref, part_ref, g2_ref, y_ref, wglu_ref, wso_ref, wmix_ref, nf_ref,
                      wfi_ref, wfo_ref, nlast_ref, out_ref, *, final):
    tm = x_ref.shape[1]
    y = jnp.concatenate([y_ref[sl].reshape(tm, LANES) for sl in range(SSM_SLABS)], axis=1)
    ys = 0.5 * y * (1.0 + jnp.tanh(math.sqrt(2.0 / math.pi) * (y + 0.044715 * (y * y * y))))
    glu = ys * _sigmoid(_dot(ys.astype(_BF16), wglu_ref[...]))
    y_ssm = _dot(glu.astype(_BF16), wso_ref[...])
    merged = part_ref[0].astype(_F32) + g2_ref[0].astype(_F32) * y_ssm
    x1 = x_ref[0] + _dot(merged.astype(_BF16), wmix_ref[...])

    h = _rmsnorm(x1, nf_ref[...]).astype(_BF16)
    gt = _dot(h, wfi_ref[:, 0:FFN_HIDDEN])
    up = _dot(h, wfi_ref[:, FFN_HIDDEN:2 * FFN_HIDDEN])
    act = (gt * _sigmoid(gt) * up).astype(_BF16)
    x2 = x1 + _dot(act, wfo_ref[...])
    if final:
        x2 = _rmsnorm(x2, nlast_ref[...])
    out_ref[0] = x2


def _mixer_out(x, part, g2, y, wglu, wso, wmix, nf, wfi, wfo, nlast, layer, final, cast_weights):
    b, l, _ = x.shape
    tm = TOKEN_TILE
    nchunk = tm // SSM_CHUNK
    grid = (b, l // tm)
    tok = pl.BlockSpec((1, tm, D_MODEL), lambda bi, li: (bi, li, 0))
    in_specs = [
        tok, tok, tok,
        _ssm_io_spec(nchunk),
        _const_spec((SSM_WIDTH, SSM_WIDTH)),
        _const_spec((SSM_WIDTH, D_MODEL)),
        _const_spec((D_MODEL, D_MODEL)),
        _layer_spec((1, D_MODEL), layer),
        _const_spec((D_MODEL, 2 * FFN_HIDDEN)),
        _const_spec((FFN_HIDDEN, D_MODEL)),
        _const_spec((1, D_MODEL)),
    ]
    cast_in, cast_out, cast_shape = _cast_plan(cast_weights, layer + 1, grid)
    body = functools.partial(_mixer_out_kernel, final=final)
    return pl.pallas_call(
        _with_casts(body, len(in_specs), 1, len(cast_weights)),
        grid=grid, in_specs=in_specs + cast_in, out_specs=[tok] + cast_out,
        out_shape=[jax.ShapeDtypeStruct((b, l, D_MODEL), _F32)] + cast_shape,
        name="mixer_out",
        compiler_params=pltpu.CompilerParams(
            dimension_semantics=("arbitrary", "arbitrary"), vmem_limit_bytes=VMEM_LIMIT),
    )(x, part, g2, y, wglu, wso, wmix, nf, wfi, wfo, nlast, *cast_weights)


def _rope_tables(seq_len):
    half = ROT_DIM // 2
    pos = jnp.arange(seq_len, dtype=_F32)
    inv_freq = ROPE_THETA ** (-jnp.arange(0, ROT_DIM, 2, dtype=_F32) / ROT_DIM)
    ang = pos[:, None] * inv_freq[None, :]
    cos, sin = jnp.cos(ang), jnp.sin(ang)
    ones = jnp.ones((seq_len, HEAD_DIM - ROT_DIM), _F32)
    zeros_h = jnp.zeros((seq_len, half), _F32)
    zeros_r = jnp.zeros((seq_len, HEAD_DIM - ROT_DIM), _F32)
    cos_h = jnp.concatenate([cos, cos, ones], axis=1)
    sa_h = jnp.concatenate([-sin, zeros_h, zeros_r], axis=1)
    sb_h = jnp.concatenate([zeros_h, sin, zeros_r], axis=1)
    rep = LANES // HEAD_DIM
    return jnp.tile(cos_h, (1, rep)), jnp.tile(sa_h, (1, rep)), jnp.tile(sb_h, (1, rep))


def kernel(x, norm_mix, w_in, b_gate, attn_sinks, w_attn_o, conv_w, w_conv_o, ssm_a_re, ssm_a_im,
           ssm_b_re, ssm_b_im, ssm_c_re, ssm_c_im, ssm_d, ssm_log_dt, w_ssm_glu, w_ssm_o, w_mix_o,
           norm_ffn, w_ffn_in, w_ffn_out, norm_final):
    b, l, d_model = x.shape
    depth = w_in.shape[0]
    assert d_model == D_MODEL and l % TOKEN_TILE == 0 and b % SUBLANES == 0
    assert l % (SSM_CHUNK * SSM_CHUNK_BLOCK) == 0
    assert w_in.shape[2] == IN_COLS and w_ffn_out.shape[1] == FFN_HIDDEN
    cos_t, sa_t, sb_t = _rope_tables(l)
    f32 = lambda p: p.astype(_F32)
    norm_mix, norm_ffn = f32(norm_mix)[:, None, :], f32(norm_ffn)[:, None, :]
    b_gate, conv_w, attn_sinks = f32(b_gate)[:, None, :], f32(conv_w), f32(attn_sinks)
    nlast = f32(norm_final).reshape(1, D_MODEL)
    wc, cct, kc, ar, ai = _ssm_operators(
        ssm_a_re, ssm_a_im, ssm_b_re, ssm_b_im, ssm_c_re, ssm_c_im, ssm_d, ssm_log_dt)
    in_weights = tuple(f32(w) for w in (w_in, w_attn_o, w_conv_o))
    out_weights = tuple(f32(w) for w in (w_ssm_glu, w_ssm_o, w_mix_o, w_ffn_in, w_ffn_out))
    w_mixer_in = [w[0].astype(_BF16) for w in in_weights]
    x = f32(x)
    for i in range(depth):
        last = i == depth - 1
        win, wao, wco = w_mixer_in
        part, g2, u, *w_mixer_out = _mixer_in(
            x, attn_sinks[i], norm_mix, win, b_gate, cos_t, sa_t, sb_t, conv_w, wao, wco, i, out_weights)
        y = _ssm(u.reshape(SSM_SLABS, l * b, LANES), wc, cct, kc, ar, ai, i, b).reshape(u.shape)
        x, *w_mixer_in = _mixer_out(
            x, part, g2, y, *w_mixer_out[:3], norm_ffn, *w_mixer_out[3:], nlast, i, last,
            () if last else in_weights)
    return x
```

```python
import functools
import math

import jax
import jax.numpy as jnp
from jax import lax
from jax.experimental import pallas as pl
from jax.experimental.pallas import tpu as pltpu

D_MODEL = 1024
N_Q_HEADS = 8
N_KV_HEADS = 2
HEAD_DIM = 64
Q_GROUP = N_Q_HEADS // N_KV_HEADS
WINDOW = 128
ROPE_THETA = 500000.0
ROT_DIM = HEAD_DIM // 4
ATTN_WIDTH = N_Q_HEADS * HEAD_DIM
KV_WIDTH = N_KV_HEADS * HEAD_DIM
NEG_INF = -1e30
CONV_WIDTH = D_MODEL // 2
CONV_K = 3
SSM_WIDTH = D_MODEL // 2
SSM_GROUP = 16
SSM_GROUPS = SSM_WIDTH // SSM_GROUP
SSM_STATE = 64
GATE_WIDTH = 3 * D_MODEL
FFN_HIDDEN = 2816
NORM_EPS = 1e-6

_Q0 = 0
_KV0 = _Q0 + ATTN_WIDTH
_CB0 = _KV0 + 2 * KV_WIDTH
_CC0 = _CB0 + CONV_WIDTH
_CX0 = _CC0 + CONV_WIDTH
_U0 = _CX0 + CONV_WIDTH
_G0 = _U0 + SSM_WIDTH
IN_COLS = _G0 + GATE_WIDTH

LANES = 128
SUBLANES = 8
BF16_ROWS = 2 * SUBLANES
SSM_CHUNK = 8
SSM_SLABS = SSM_WIDTH // LANES
SLAB_GROUPS = LANES // SSM_GROUP
SLAB_STATE = SLAB_GROUPS * SSM_STATE
TOKEN_TILE = 512
SSM_CHUNK_BLOCK = 128
MATMUL_ROWS = 256
VMEM_LIMIT = 56 * 1024 * 1024

_BF16 = jnp.bfloat16
_F32 = jnp.float32


def _dot(a, b):
    m = a.shape[0]
    if m <= MATMUL_ROWS:
        return jnp.dot(a, b, preferred_element_type=_F32)
    return jnp.concatenate(
        [jnp.dot(a[i:i + MATMUL_ROWS], b, preferred_element_type=_F32) for i in range(0, m, MATMUL_ROWS)], axis=0)


def _sigmoid(x):
    return 0.5 * jnp.tanh(0.5 * x) + 0.5


def _rmsnorm(x, w):
    ms = jnp.mean(x * x, axis=-1, keepdims=True)
    return x * lax.rsqrt(ms + NORM_EPS) * w


def _const_spec(shape):
    return pl.BlockSpec(shape, lambda *_: (0,) * len(shape), pipeline_mode=pl.Buffered(1))


def _layer_spec(shape, layer):
    return pl.BlockSpec((None,) + shape, lambda *_: (layer,) + (0,) * len(shape), pipeline_mode=pl.Buffered(1))


def _cast_plan(weights, layer, nsteps, step_of):
    in_specs, out_specs, out_shapes = [], [], []
    for w in weights:
        _, rows, cols = w.shape
        ncol = next(n for n in (1, 2, 4, 8)
                    if (rows * n) % (nsteps * BF16_ROWS) == 0 and cols % (n * LANES) == 0)
        nrow = nsteps // ncol
        block = (rows // nrow, cols // ncol)

        def index(*ids, nrow=nrow):
            step = step_of(*ids)
            return step % nrow, step // nrow

        in_specs.append(pl.BlockSpec((None,) + block, lambda *ids, index=index: (layer,) + index(*ids)))
        out_specs.append(pl.BlockSpec(block, index))
        out_shapes.append(jax.ShapeDtypeStruct((rows, cols), _BF16))
    return in_specs, out_specs, out_shapes


def _with_casts(body, n_in, n_out, n_cast):
    def kernel(*refs):
        ins, rest = refs[:n_in], refs[n_in:]
        cast_src, rest = rest[:n_cast], rest[n_cast:]
        outs, rest = rest[:n_out], rest[n_out:]
        cast_dst, scratch = rest[:n_cast], rest[n_cast:]
        body(*ins, *outs, *scratch)
        for src, dst in zip(cast_src, cast_dst):
            dst[...] = src[...].astype(_BF16)
    return kernel


class _Skew:
    def __init__(self, batch, seq_len):
        self.tiles_per_seq = seq_len // TOKEN_TILE
        self.steps = batch * self.tiles_per_seq
        self.grid = (self.steps + 1,)

    def started(self, s):
        return jnp.minimum(s, self.steps - 1)

    def finished(self, s):
        return jnp.maximum(s - 1, 0)

    def tokens(self, tile):
        n = self.tiles_per_seq
        return pl.BlockSpec((1, TOKEN_TILE, D_MODEL), lambda s: (tile(s) // n, tile(s) % n, 0))

    def ssm_io(self):
        n = self.tiles_per_seq
        return pl.BlockSpec((SSM_SLABS, TOKEN_TILE // SSM_CHUNK, None, SSM_CHUNK, LANES),
                            lambda s: (0, self.started(s) % n, self.started(s) // n, 0, 0))


def _mixer_in_kernel(sinks_ref, x_ref, nw_ref, win_ref, bg_ref, cos_ref, sa_ref, sb_ref, cw_ref,
                     wao_ref, wco_ref, part_ref, g2_ref, u_ref,
                     kv_scr, z_scr, attn_scr, yc_scr, acc_scr, g0_scr, *, steps, tiles_per_seq):
    step = pl.program_id(0)
    li = step % tiles_per_seq

    def finish():
        y_attn = _dot(attn_scr[...], wao_ref[...])
        part_ref[0] = (g0_scr[...] * y_attn + acc_scr[...]).astype(_BF16)

    @pl.when(step == 0)
    def _():
        attn_scr[...] = jnp.zeros_like(attn_scr)
        acc_scr[...] = jnp.zeros_like(acc_scr)
        g0_scr[...] = jnp.zeros_like(g0_scr)

    @pl.when((step < steps) & (li == 0))
    def _():
        kv_scr[:, 0:WINDOW, :] = jnp.zeros((8, WINDOW, LANES), _BF16)
        z_scr[0:SUBLANES, :] = jnp.zeros((SUBLANES, CONV_WIDTH), _F32)

    @pl.when(step == steps)
    def _():
        finish()

    @pl.when(step < steps)
    def _():
        finish()
        _mixer_in_start(sinks_ref, x_ref, nw_ref, win_ref, bg_ref, cos_ref, sa_ref, sb_ref, cw_ref, wco_ref,
                        g2_ref, u_ref, kv_scr, z_scr, attn_scr, yc_scr, acc_scr, g0_scr, li)


def _mixer_in_start(sinks_ref, x_ref, nw_ref, win_ref, bg_ref, cos_ref, sa_ref, sb_ref, cw_ref, wco_ref,
                    g2_ref, u_ref, kv_scr, z_scr, attn_scr, yc_scr, acc_scr, g0_scr, li):
    tm = x_ref.shape[1]
    x = x_ref[0]
    h = _rmsnorm(x, nw_ref[...]).astype(_BF16)

    cos = cos_ref[...]
    sa = sa_ref[...]
    sb = sb_ref[...]

    def rope(t):
        return (t * cos + pltpu.roll(t, LANES - ROT_DIM // 2, axis=1) * sa
                + pltpu.roll(t, ROT_DIM // 2, axis=1) * sb)

    kv = _dot(h, win_ref[:, _KV0:_KV0 + 2 * KV_WIDTH])
    lane = lax.broadcasted_iota(jnp.int32, (tm, LANES), 1)
    lo = lane < HEAD_DIM
    for kind in range(2):
        t = kv[:, kind * LANES:(kind + 1) * LANES]
        if kind == 0:
            t = rope(t)
        t_sw = pltpu.roll(t, HEAD_DIM, axis=1)
        zero = jnp.zeros_like(t)
        variants = (jnp.where(lo, t, zero), jnp.where(lo, zero, t_sw),
                    jnp.where(lo, t_sw, zero), jnp.where(lo, zero, t))
        for n, val in enumerate(variants):
            kv_scr[kind * 4 + n, WINDOW:WINDOW + tm, :] = val.astype(_BF16)

    q = _dot(h, win_ref[:, _Q0:_Q0 + ATTN_WIDTH])
    row = lax.broadcasted_iota(jnp.int32, (WINDOW, 2 * WINDOW), 0)
    col = lax.broadcasted_iota(jnp.int32, (WINDOW, 2 * WINDOW), 1)
    band_ok = (col > row) & (col <= row + WINDOW)
    first_ok = band_ok & (col >= jnp.where(li > 0, 0, WINDOW))

    def gate(n):
        c0 = _G0 + n * D_MODEL
        return _sigmoid(_dot(h, win_ref[:, c0:c0 + D_MODEL]) + bg_ref[:, n * D_MODEL:(n + 1) * D_MODEL])


    cb = _dot(h, win_ref[:, _CB0:_CB0 + CONV_WIDTH])
    cc = _dot(h, win_ref[:, _CC0:_CC0 + CONV_WIDTH])
    cx = _dot(h, win_ref[:, _CX0:_CX0 + CONV_WIDTH])
    z = cc * cx
    z_scr[SUBLANES:SUBLANES + tm, :] = z
    z1 = z_scr[SUBLANES - 1:SUBLANES - 1 + tm, :]
    z2 = z_scr[SUBLANES - 2:SUBLANES - 2 + tm, :]
    conv = cw_ref[0:1, :] * z2 + cw_ref[1:2, :] * z1 + cw_ref[2:3, :] * z
    yc_scr[...] = (cb * conv).astype(_BF16)
    acc_scr[...] = gate(1) * _dot(yc_scr[...], wco_ref[...])

    g0_scr[...] = gate(0)
    u = _dot(h, win_ref[:, _U0:_U0 + SSM_WIDTH])
    nchunk = tm // SSM_CHUNK
    for sl in range(SSM_SLABS):
        u_ref[sl] = u[:, sl * LANES:(sl + 1) * LANES].reshape(nchunk, SSM_CHUNK, LANES)

    for m in range(N_Q_HEADS // 2):
        qp = (rope(q[:, m * LANES:(m + 1) * LANES]) * (HEAD_DIM ** -0.5)).astype(_BF16)
        j = (2 * m) // Q_GROUP
        for r in range(tm // WINDOW):
            qb = qp[r * WINDOW:(r + 1) * WINDOW]
            mask = first_ok if r == 0 else band_ok
            o = None
            for half in range(2):
                sink = sinks_ref[2 * m + half]
                kb = kv_scr[j * 2 + half, r * WINDOW:(r + 2) * WINDOW, :]
                vb = kv_scr[4 + j * 2 + half, r * WINDOW:(r + 2) * WINDOW, :]
                s = lax.dot_general(qb, kb, (((1,), (1,)), ((), ())), preferred_element_type=_F32)
                s = jnp.where(mask, s, NEG_INF)
                mx = jnp.maximum(jnp.max(s, axis=-1, keepdims=True), sink)
                p = jnp.exp(s - mx)
                denom = jnp.sum(p, axis=-1, keepdims=True) + jnp.exp(sink - mx)
                oh = _dot(p.astype(_BF16), vb) * (1.0 / denom)
                o = oh if o is None else o + oh
            attn_scr[r * WINDOW:(r + 1) * WINDOW, m * LANES:(m + 1) * LANES] = o.astype(_BF16)

    g2_ref[0] = gate(2).astype(_BF16)

    kv_scr[:, 0:WINDOW, :] = kv_scr[:, tm:tm + WINDOW, :]
    z_scr[0:SUBLANES, :] = z_scr[tm:tm + SUBLANES, :]


def _mixer_in(x, sinks, nw, win, bg, cos_t, sa_t, sb_t, cw, wao, wco, layer, cast_weights):
    b, l, _ = x.shape
    tm = TOKEN_TILE
    skew = _Skew(b, l)
    pos = pl.BlockSpec((tm, LANES), lambda s: (skew.started(s) % skew.tiles_per_seq, 0))
    in_specs = [
        pl.BlockSpec(memory_space=pltpu.SMEM),
        skew.tokens(skew.started),
        _layer_spec((1, D_MODEL), layer),
        _const_spec((D_MODEL, IN_COLS)),
        _layer_spec((1, GATE_WIDTH), layer),
        pos, pos, pos,
        _layer_spec((CONV_K, CONV_WIDTH), layer),
        _const_spec((ATTN_WIDTH, D_MODEL)),
        _const_spec((CONV_WIDTH, D_MODEL)),
    ]
    out_specs = [
        skew.tokens(skew.finished),
        skew.tokens(skew.started),
        skew.ssm_io(),
    ]
    out_shape = [
        jax.ShapeDtypeStruct((b, l, D_MODEL), _BF16),
        jax.ShapeDtypeStruct((b, l, D_MODEL), _BF16),
        jax.ShapeDtypeStruct((SSM_SLABS, l // SSM_CHUNK, b, SSM_CHUNK, LANES), _F32),
    ]
    scratch = [
        pltpu.VMEM((8, WINDOW + tm, LANES), _BF16),
        pltpu.VMEM((SUBLANES + tm, CONV_WIDTH), _F32),
        pltpu.VMEM((tm, ATTN_WIDTH), _BF16),
        pltpu.VMEM((tm, CONV_WIDTH), _BF16),
        pltpu.VMEM((tm, D_MODEL), _F32),
        pltpu.VMEM((tm, D_MODEL), _F32),
    ]
    cast_in, cast_out, cast_shape = _cast_plan(cast_weights, layer, skew.steps, skew.started)
    body = functools.partial(_mixer_in_kernel, steps=skew.steps, tiles_per_seq=skew.tiles_per_seq)
    return pl.pallas_call(
        _with_casts(body, len(in_specs), len(out_specs), len(cast_weights)),
        grid=skew.grid, in_specs=in_specs + cast_in, out_specs=out_specs + cast_out,
        out_shape=out_shape + cast_shape, scratch_shapes=scratch, name="mixer_in",
        compiler_params=pltpu.CompilerParams(
            dimension_semantics=("arbitrary",), vmem_limit_bytes=VMEM_LIMIT),
    )(sinks, x, nw, win, bg, cos_t, sa_t, sb_t, cw, wao, wco, *cast_weights)


def _ssm_kernel(u_ref, wc_ref, cct_ref, kc_ref, ar_ref, ai_ref, y_ref,
                mb_scr, mt_scr, mct_scr, ucat_scr, s_scr, xr_scr, xi_scr, *, batch):
    t = SSM_CHUNK
    rows = u_ref.shape[1] // t

    @pl.when(pl.program_id(1) == 0)
    def _():
        chan_bits, state_bits = SSM_GROUP.bit_length() - 1, SSM_STATE.bit_length() - 1
        r_grp = lax.broadcasted_iota(jnp.int32, (LANES, 2 * SLAB_STATE), 0) >> chan_bits
        c_grp = (lax.broadcasted_iota(jnp.int32, (LANES, 2 * SLAB_STATE), 1) & (SLAB_STATE - 1)) >> state_bits
        same_state = r_grp == c_grp

        def expand(tbl):
            return jnp.where(same_state, jnp.tile(tbl, (SLAB_GROUPS, 1)), jnp.zeros((), tbl.dtype))

        r_grp = lax.broadcasted_iota(jnp.int32, (LANES, LANES), 0) >> chan_bits
        c_grp = lax.broadcasted_iota(jnp.int32, (LANES, LANES), 1) >> chan_bits
        same_chan = r_grp == c_grp
        zero_blk = jnp.zeros((LANES, LANES), _BF16)
        blks = []
        for st in range(t):
            rs = slice(st * LANES, (st + 1) * LANES)
            mb_scr[rs, :] = expand(wc_ref[0, st]).astype(_BF16)
            mct_scr[rs, :] = expand(cct_ref[0, st]).astype(_BF16)
            blks.append(jnp.where(same_chan, jnp.tile(kc_ref[0, st], (SLAB_GROUPS, 1)), 0.0).astype(_BF16))
        for s_in in range(t):
            for s_out in range(t):
                mt_scr[s_in * LANES:(s_in + 1) * LANES, s_out * LANES:(s_out + 1) * LANES] = (
                    blks[s_out - s_in] if s_out >= s_in else zero_blk)
        xr_scr[...] = jnp.zeros_like(xr_scr)
        xi_scr[...] = jnp.zeros_like(xi_scr)

    for st in range(t):
        ucat_scr[:, st * LANES:(st + 1) * LANES] = u_ref[0, pl.ds(st, rows, stride=t), :].astype(_BF16)

    s_scr[...] = _dot(ucat_scr[...], mb_scr[...])

    ar = jnp.broadcast_to(ar_ref[0], (batch, SLAB_STATE))
    ai = jnp.broadcast_to(ai_ref[0], (batch, SLAB_STATE))

    xr, xi = xr_scr[...], xi_scr[...]
    for c in range(rows // batch):
        rs = slice(c * batch, (c + 1) * batch)
        sr = s_scr[rs, 0:SLAB_STATE]
        si = s_scr[rs, SLAB_STATE:2 * SLAB_STATE]
        s_scr[rs, 0:SLAB_STATE] = xr
        s_scr[rs, SLAB_STATE:2 * SLAB_STATE] = xi
        xr, xi = ar * xr - ai * xi + sr, ar * xi + ai * xr + si
    xr_scr[...] = xr
    xi_scr[...] = xi

    pair = 2 * LANES
    toeplitz = jnp.concatenate(
        [_dot(ucat_scr[:, 0:c0 + pair], mt_scr[0:c0 + pair, c0:c0 + pair]) for c0 in range(0, t * LANES, pair)],
        axis=1)
    y = toeplitz + lax.dot_general(
        s_scr[...].astype(_BF16), mct_scr[...], (((1,), (1,)), ((), ())), preferred_element_type=_F32)
    for st in range(t):
        y_ref[0, pl.ds(st, rows, stride=t), :] = y[:, st * LANES:(st + 1) * LANES]


def _ssm(u, wc, cct, kc, ar, ai, layer, batch):
    nslab, tok_rows, _ = u.shape
    blk = SSM_CHUNK_BLOCK * batch * SSM_CHUNK
    rows = SSM_CHUNK_BLOCK * batch
    width = SSM_CHUNK * LANES
    io = pl.BlockSpec((1, blk, LANES), lambda q, ci: (q, ci, 0))
    par = lambda *shape: pl.BlockSpec((None, 1) + shape, lambda q, ci: (layer, q) + (0,) * len(shape))
    return pl.pallas_call(
        functools.partial(_ssm_kernel, batch=batch),
        grid=(nslab, tok_rows // blk),
        in_specs=[io, par(SSM_CHUNK, SSM_GROUP, 2 * SLAB_STATE), par(SSM_CHUNK, SSM_GROUP, 2 * SLAB_STATE),
                  par(SSM_CHUNK, SSM_GROUP, LANES), par(1, SLAB_STATE), par(1, SLAB_STATE)],
        out_specs=io,
        out_shape=jax.ShapeDtypeStruct(u.shape, _F32),
        scratch_shapes=[
            pltpu.VMEM((width, 2 * SLAB_STATE), _BF16),
            pltpu.VMEM((width, width), _BF16),
            pltpu.VMEM((width, 2 * SLAB_STATE), _BF16),
            pltpu.VMEM((rows, width), _BF16),
            pltpu.VMEM((rows, 2 * SLAB_STATE), _F32),
            pltpu.VMEM((batch, SLAB_STATE), _F32),
            pltpu.VMEM((batch, SLAB_STATE), _F32),
        ],
        name="ssm",
        compiler_params=pltpu.CompilerParams(
            dimension_semantics=("arbitrary", "arbitrary"), vmem_limit_bytes=VMEM_LIMIT),
    )(u, wc, cct, kc, ar, ai)


def _ssm_operators(a_re, a_im, b_re, b_im, c_re, c_im, d, log_dt):
    t = SSM_CHUNK
    p, hh, nq, ng = SSM_STATE, SSM_GROUP, SSM_SLABS, SLAB_GROUPS
    depth = a_re.shape[0]
    hi = lax.Precision.HIGHEST
    a_re, a_im = a_re.astype(_F32), a_im.astype(_F32)
    dt = jnp.exp(log_dt.astype(_F32))[:, :, None]
    b_re, b_im = b_re.astype(_F32), b_im.astype(_F32)
    c_re, c_im = c_re.astype(_F32), c_im.astype(_F32)

    def powers(ar_, ai_, dt_, n):
        mag, ang = jnp.exp(ar_ * dt_ * n), ai_ * dt_ * n
        return mag * jnp.cos(ang), mag * jnp.sin(ang)

    def zoh(ar_, ai_, dt_):
        lr, li_ = powers(ar_, ai_, dt_, 1.0)
        nr, den = lr - 1.0, ar_ * ar_ + ai_ * ai_
        return (nr * ar_ + li_ * ai_) / den, (li_ * ar_ - nr * ai_) / den

    steps = jnp.arange(t, dtype=_F32)[:, None, None]
    pr, pi = powers(a_re[:, :, None, None], a_im[:, :, None, None], dt[:, :, None, None], steps)
    cr, ci = c_re[:, :, None], c_im[:, :, None]
    lhs = jnp.concatenate([cr * pr - ci * pi, -(cr * pi + ci * pr)], axis=-1)
    f_re, f_im = zoh(a_re, a_im, dt)
    rhs = jnp.concatenate([f_re[..., None] * b_re - f_im[..., None] * b_im,
                           f_re[..., None] * b_im + f_im[..., None] * b_re], axis=-2)
    kern = jnp.einsum("ygmc,ygck->ygmk", lhs.reshape(depth, -1, t * hh, 2 * p), rhs, precision=hi)
    kern = kern.reshape(depth, -1, t, hh, hh)
    skip = jnp.eye(hh, dtype=_F32) * d.astype(_F32).reshape(depth, -1, hh, 1)
    kern = jnp.concatenate([kern[:, :, :1] + skip[:, :, None], kern[:, :, 1:]], axis=2)
    kc = kern.reshape(depth, nq, ng, t, hh, hh).transpose(0, 1, 3, 5, 2, 4).reshape(depth, nq, t, hh, LANES)

    lane = lambda v: v.reshape(depth, nq, 1, ng * p)
    la_re, la_im = lane(a_re), lane(a_im)
    ldt = lane(jnp.broadcast_to(dt, a_re.shape))
    lf_re, lf_im = zoh(la_re, la_im, ldt)
    to_rows = lambda v, axes: v.reshape(depth, nq, ng, *v.shape[2:]).transpose(axes).reshape(depth, nq, hh, ng * p)
    lb_re, lb_im = to_rows(b_re, (0, 1, 4, 2, 3)), to_rows(b_im, (0, 1, 4, 2, 3))
    lc_re, lc_im = to_rows(c_re, (0, 1, 3, 2, 4)), to_rows(c_im, (0, 1, 3, 2, 4))
    bb_re, bb_im = lf_re * lb_re - lf_im * lb_im, lf_re * lb_im + lf_im * lb_re

    rev = (t - 1 - jnp.arange(t, dtype=_F32))[:, None, None]
    rr, ri = powers(la_re[:, :, None], la_im[:, :, None], ldt[:, :, None], rev)
    wc = jnp.concatenate([rr * bb_re[:, :, None] - ri * bb_im[:, :, None],
                          rr * bb_im[:, :, None] + ri * bb_re[:, :, None]], axis=-1)
    fwd = (1.0 + jnp.arange(t, dtype=_F32))[:, None, None]
    fr, fi = powers(la_re[:, :, None], la_im[:, :, None], ldt[:, :, None], fwd)
    cct = jnp.concatenate([lc_re[:, :, None] * fr - lc_im[:, :, None] * fi,
                           -(lc_re[:, :, None] * fi + lc_im[:, :, None] * fr)], axis=-1)
    ar, ai = powers(la_re, la_im, ldt, float(t))
    return wc.astype(_BF16), cct.astype(_BF16), kc, ar, ai


def _mixer_out_kernel(x_ref, part_ref, g2_ref, y_ref, wglu_ref, wso_ref, wmix_ref, nf_ref,
                      wfi_ref, wfo_ref, nlast_ref, out_ref, act_scr, x1_scr, *, final, steps):
    tm = x_ref.shape[1]
    step = pl.program_id(0)

    def finish():
        x2 = x1_scr[...] + _dot(act_scr[...], wfo_ref[...])
        if final:
            x2 = _rmsnorm(x2, nlast_ref[...])
        out_ref[0] = x2

    def start():
        y = jnp.concatenate([y_ref[sl].reshape(tm, LANES) for sl in range(SSM_SLABS)], axis=1)
        ys = 0.5 * y * (1.0 + jnp.tanh(math.sqrt(2.0 / math.pi) * (y + 0.044715 * (y * y * y))))
        glu = ys * _sigmoid(_dot(ys.astype(_BF16), wglu_ref[...]))
        y_ssm = _dot(glu.astype(_BF16), wso_ref[...])
        merged = part_ref[0].astype(_F32) + g2_ref[0].astype(_F32) * y_ssm
        x1 = x_ref[0] + _dot(merged.astype(_BF16), wmix_ref[...])
        h = _rmsnorm(x1, nf_ref[...]).astype(_BF16)
        gt = _dot(h, wfi_ref[:, 0:FFN_HIDDEN])
        up = _dot(h, wfi_ref[:, FFN_HIDDEN:2 * FFN_HIDDEN])
        x1_scr[...] = x1
        act_scr[...] = (gt * _sigmoid(gt) * up).astype(_BF16)

    @pl.when(step == 0)
    def _():
        act_scr[...] = jnp.zeros_like(act_scr)
        x1_scr[...] = jnp.zeros_like(x1_scr)

    @pl.when(step == steps)
    def _():
        finish()

    @pl.when(step < steps)
    def _():
        finish()
        start()


def _mixer_out(x, part, g2, y, wglu, wso, wmix, nf, wfi, wfo, nlast, layer, final, cast_weights):
    b, l, _ = x.shape
    skew = _Skew(b, l)
    tok = skew.tokens(skew.started)
    in_specs = [
        tok, tok, tok,
        skew.ssm_io(),
        _const_spec((SSM_WIDTH, SSM_WIDTH)),
        _const_spec((SSM_WIDTH, D_MODEL)),
        _const_spec((D_MODEL, D_MODEL)),
        _layer_spec((1, D_MODEL), layer),
        _const_spec((D_MODEL, 2 * FFN_HIDDEN)),
        _const_spec((FFN_HIDDEN, D_MODEL)),
        _const_spec((1, D_MODEL)),
    ]
    cast_in, cast_out, cast_shape = _cast_plan(cast_weights, layer + 1, skew.steps, skew.started)
    body = functools.partial(_mixer_out_kernel, final=final, steps=skew.steps)
    return pl.pallas_call(
        _with_casts(body, len(in_specs), 1, len(cast_weights)),
        grid=skew.grid, in_specs=in_specs + cast_in, out_specs=[skew.tokens(skew.finished)] + cast_out,
        out_shape=[jax.ShapeDtypeStruct((b, l, D_MODEL), _F32)] + cast_shape,
        scratch_shapes=[pltpu.VMEM((TOKEN_TILE, FFN_HIDDEN), _BF16), pltpu.VMEM((TOKEN_TILE, D_MODEL), _F32)],
        name="mixer_out",
        compiler_params=pltpu.CompilerParams(
            dimension_semantics=("arbitrary",), vmem_limit_bytes=VMEM_LIMIT),
    )(x, part, g2, y, wglu, wso, wmix, nf, wfi, wfo, nlast, *cast_weights)


def _rope_tables(seq_len):
    half = ROT_DIM // 2
    pos = jnp.arange(seq_len, dtype=_F32)
    inv_freq = ROPE_THETA ** (-jnp.arange(0, ROT_DIM, 2, dtype=_F32) / ROT_DIM)
    ang = pos[:, None] * inv_freq[None, :]
    cos, sin = jnp.cos(ang), jnp.sin(ang)
    ones = jnp.ones((seq_len, HEAD_DIM - ROT_DIM), _F32)
    zeros_h = jnp.zeros((seq_len, half), _F32)
    zeros_r = jnp.zeros((seq_len, HEAD_DIM - ROT_DIM), _F32)
    cos_h = jnp.concatenate([cos, cos, ones], axis=1)
    sa_h = jnp.concatenate([-sin, zeros_h, zeros_r], axis=1)
    sb_h = jnp.concatenate([zeros_h, sin, zeros_r], axis=1)
    rep = LANES // HEAD_DIM
    return jnp.tile(cos_h, (1, rep)), jnp.tile(sa_h, (1, rep)), jnp.tile(sb_h, (1, rep))


def kernel(x, norm_mix, w_in, b_gate, attn_sinks, w_attn_o, conv_w, w_conv_o, ssm_a_re, ssm_a_im,
           ssm_b_re, ssm_b_im, ssm_c_re, ssm_c_im, ssm_d, ssm_log_dt, w_ssm_glu, w_ssm_o, w_mix_o,
           norm_ffn, w_ffn_in, w_ffn_out, norm_final):
    b, l, d_model = x.shape
    depth = w_in.shape[0]
    assert d_model == D_MODEL and l % TOKEN_TILE == 0 and b % SUBLANES == 0
    assert l % (SSM_CHUNK * SSM_CHUNK_BLOCK) == 0
    assert w_in.shape[2] == IN_COLS and w_ffn_out.shape[1] == FFN_HIDDEN
    cos_t, sa_t, sb_t = _rope_tables(l)
    f32 = lambda p: p.astype(_F32)
    norm_mix, norm_ffn = f32(norm_mix)[:, None, :], f32(norm_ffn)[:, None, :]
    b_gate, conv_w, attn_sinks = f32(b_gate)[:, None, :], f32(conv_w), f32(attn_sinks)
    nlast = f32(norm_final).reshape(1, D_MODEL)
    wc, cct, kc, ar, ai = _ssm_operators(
        ssm_a_re, ssm_a_im, ssm_b_re, ssm_b_im, ssm_c_re, ssm_c_im, ssm_d, ssm_log_dt)
    in_weights = tuple(f32(w) for w in (w_in, w_attn_o, w_conv_o))
    out_weights = tuple(f32(w) for w in (w_ssm_glu, w_ssm_o, w_mix_o, w_ffn_in, w_ffn_out))
    w_mixer_in = [w[0].astype(_BF16) for w in in_weights]
    x = f32(x)
    for i in range(depth):
        last = i == depth - 1
        win, wao, wco = w_mixer_in
        part, g2, u, *w_mixer_out = _mixer_in(
            x, attn_sinks[i], norm_mix, win, b_gate, cos_t, sa_t, sb_t, conv_w, wao, wco, i, out_weights)
        y = _ssm(u.reshape(SSM_SLABS, l * b, LANES), wc, cct, kc, ar, ai, i, b).reshape(u.shape)
        x, *w_mixer_in = _mixer_out(
            x, part, g2, y, *w_mixer_out[:3], norm_ffn, *w_mixer_out[3:], nlast, i, last,
            () if last else in_weights)
    return x
```

```python
import functools
import math

import jax
import jax.numpy as jnp
import numpy as np
from jax import lax
from jax.experimental import pallas as pl
from jax.experimental.pallas import tpu as pltpu

D_MODEL = 1024
N_Q_HEADS = 8
N_KV_HEADS = 2
HEAD_DIM = 64
Q_GROUP = N_Q_HEADS // N_KV_HEADS
WINDOW = 128
ROPE_THETA = 500000.0
ROT_DIM = HEAD_DIM // 4
ATTN_WIDTH = N_Q_HEADS * HEAD_DIM
KV_WIDTH = N_KV_HEADS * HEAD_DIM
NEG_INF = -1e30
CONV_WIDTH = D_MODEL // 2
CONV_K = 3
SSM_WIDTH = D_MODEL // 2
SSM_GROUP = 16
SSM_GROUPS = SSM_WIDTH // SSM_GROUP
SSM_STATE = 64
GATE_WIDTH = 3 * D_MODEL
FFN_HIDDEN = 2816
NORM_EPS = 1e-6

_Q0 = 0
_KV0 = _Q0 + ATTN_WIDTH
_CB0 = _KV0 + 2 * KV_WIDTH
_CC0 = _CB0 + CONV_WIDTH
_CX0 = _CC0 + CONV_WIDTH
_U0 = _CX0 + CONV_WIDTH
_G0 = _U0 + SSM_WIDTH
IN_COLS = _G0 + GATE_WIDTH

LANES = 128
SUBLANES = 8
BF16_ROWS = 2 * SUBLANES
SSM_CHUNK = 8
SSM_SLABS = SSM_WIDTH // LANES
SLAB_GROUPS = LANES // SSM_GROUP
SLAB_STATE = SLAB_GROUPS * SSM_STATE
TOKEN_TILE = 512
SSM_CHUNK_BLOCK = 128
MATMUL_ROWS = 256
VMEM_LIMIT = 56 * 1024 * 1024

_BF16 = jnp.bfloat16
_F32 = jnp.float32


def _dot(a, b):
    m = a.shape[0]
    if m <= MATMUL_ROWS:
        return jnp.dot(a, b, preferred_element_type=_F32)
    return jnp.concatenate(
        [jnp.dot(a[i:i + MATMUL_ROWS], b, preferred_element_type=_F32) for i in range(0, m, MATMUL_ROWS)], axis=0)


def _sigmoid(x):
    return 0.5 * jnp.tanh(0.5 * x) + 0.5


def _rmsnorm(x, w):
    ms = jnp.mean(x * x, axis=-1, keepdims=True)
    return x * lax.rsqrt(ms + NORM_EPS) * w


def _const_spec(shape):
    return pl.BlockSpec(shape, lambda *_: (0,) * len(shape), pipeline_mode=pl.Buffered(1))


def _layer_spec(shape, layer):
    return pl.BlockSpec((None,) + shape, lambda *_: (layer,) + (0,) * len(shape), pipeline_mode=pl.Buffered(1))


def _cast_plan(weights, layer, grid):
    nsteps = grid[0] * grid[1]
    in_specs, out_specs, out_shapes = [], [], []
    for w in weights:
        _, rows, cols = w.shape
        ncol = next(n for n in (1, 2, 4, 8)
                    if (rows * n) % (nsteps * BF16_ROWS) == 0 and cols % (n * LANES) == 0)
        nrow = nsteps // ncol
        block = (rows // nrow, cols // ncol)

        def index(bi, li, nrow=nrow):
            step = bi * grid[1] + li
            return step % nrow, step // nrow

        in_specs.append(pl.BlockSpec((None,) + block, lambda bi, li, index=index: (layer,) + index(bi, li)))
        out_specs.append(pl.BlockSpec(block, index))
        out_shapes.append(jax.ShapeDtypeStruct((rows, cols), _BF16))
    return in_specs, out_specs, out_shapes


def _with_casts(body, n_in, n_out, n_cast):
    def kernel(*refs):
        ins, rest = refs[:n_in], refs[n_in:]
        cast_src, rest = rest[:n_cast], rest[n_cast:]
        outs, rest = rest[:n_out], rest[n_out:]
        cast_dst, scratch = rest[:n_cast], rest[n_cast:]
        body(*ins, *outs, *scratch)
        for src, dst in zip(cast_src, cast_dst):
            dst[...] = src[...].astype(_BF16)
    return kernel


def _ssm_io_spec(nchunk):
    return pl.BlockSpec((SSM_SLABS, nchunk, None, SSM_CHUNK, LANES), lambda bi, li: (0, li, bi, 0, 0))


def _mixer_in_kernel(sinks_ref, x_ref, nw_ref, win_ref, bg_ref, cos_ref, sa_ref, sb_ref, cw_ref,
                     wao_ref, wco_ref, part_ref, g2_ref, u_ref,
                     kv_scr, z_scr, attn_scr, yc_scr, acc_scr, g0_scr):
    tm = x_ref.shape[1]
    li = pl.program_id(1)

    @pl.when(li == 0)
    def _():
        kv_scr[:, 0:WINDOW, :] = jnp.zeros((8, WINDOW, LANES), _BF16)
        z_scr[0:SUBLANES, :] = jnp.zeros((SUBLANES, CONV_WIDTH), _F32)

    x = x_ref[0]
    h = _rmsnorm(x, nw_ref[...]).astype(_BF16)

    cos = cos_ref[...]
    sa = sa_ref[...]
    sb = sb_ref[...]

    def rope(t):
        return (t * cos + pltpu.roll(t, LANES - ROT_DIM // 2, axis=1) * sa
                + pltpu.roll(t, ROT_DIM // 2, axis=1) * sb)

    kv = _dot(h, win_ref[:, _KV0:_KV0 + 2 * KV_WIDTH])
    lane = lax.broadcasted_iota(jnp.int32, (tm, LANES), 1)
    lo = lane < HEAD_DIM
    for kind in range(2):
        t = kv[:, kind * LANES:(kind + 1) * LANES]
        if kind == 0:
            t = rope(t)
        t_sw = pltpu.roll(t, HEAD_DIM, axis=1)
        zero = jnp.zeros_like(t)
        variants = (jnp.where(lo, t, zero), jnp.where(lo, zero, t_sw),
                    jnp.where(lo, t_sw, zero), jnp.where(lo, zero, t))
        for n, val in enumerate(variants):
            kv_scr[kind * 4 + n, WINDOW:WINDOW + tm, :] = val.astype(_BF16)

    q = _dot(h, win_ref[:, _Q0:_Q0 + ATTN_WIDTH])
    row = lax.broadcasted_iota(jnp.int32, (WINDOW, 2 * WINDOW), 0)
    col = lax.broadcasted_iota(jnp.int32, (WINDOW, 2 * WINDOW), 1)
    band_ok = (col > row) & (col <= row + WINDOW)
    first_ok = band_ok & (col >= jnp.where(li > 0, 0, WINDOW))

    def gate(n):
        c0 = _G0 + n * D_MODEL
        return _sigmoid(_dot(h, win_ref[:, c0:c0 + D_MODEL]) + bg_ref[:, n * D_MODEL:(n + 1) * D_MODEL])


    cb = _dot(h, win_ref[:, _CB0:_CB0 + CONV_WIDTH])
    cc = _dot(h, win_ref[:, _CC0:_CC0 + CONV_WIDTH])
    cx = _dot(h, win_ref[:, _CX0:_CX0 + CONV_WIDTH])
    z = cc * cx
    z_scr[SUBLANES:SUBLANES + tm, :] = z
    z1 = z_scr[SUBLANES - 1:SUBLANES - 1 + tm, :]
    z2 = z_scr[SUBLANES - 2:SUBLANES - 2 + tm, :]
    conv = cw_ref[0:1, :] * z2 + cw_ref[1:2, :] * z1 + cw_ref[2:3, :] * z
    yc_scr[...] = (cb * conv).astype(_BF16)
    acc_scr[...] = gate(1) * _dot(yc_scr[...], wco_ref[...])

    g0_scr[...] = gate(0)
    u = _dot(h, win_ref[:, _U0:_U0 + SSM_WIDTH])
    nchunk = tm // SSM_CHUNK
    for sl in range(SSM_SLABS):
        u_ref[sl] = u[:, sl * LANES:(sl + 1) * LANES].reshape(nchunk, SSM_CHUNK, LANES)

    for m in range(N_Q_HEADS // 2):
        qp = (rope(q[:, m * LANES:(m + 1) * LANES]) * (HEAD_DIM ** -0.5)).astype(_BF16)
        j = (2 * m) // Q_GROUP
        for r in range(tm // WINDOW):
            qb = qp[r * WINDOW:(r + 1) * WINDOW]
            mask = first_ok if r == 0 else band_ok
            o = None
            for half in range(2):
                sink = sinks_ref[2 * m + half]
                kb = kv_scr[j * 2 + half, r * WINDOW:(r + 2) * WINDOW, :]
                vb = kv_scr[4 + j * 2 + half, r * WINDOW:(r + 2) * WINDOW, :]
                s = lax.dot_general(qb, kb, (((1,), (1,)), ((), ())), preferred_element_type=_F32)
                s = jnp.where(mask, s, NEG_INF)
                mx = jnp.maximum(jnp.max(s, axis=-1, keepdims=True), sink)
                p = jnp.exp(s - mx)
                denom = jnp.sum(p, axis=-1, keepdims=True) + jnp.exp(sink - mx)
                oh = _dot(p.astype(_BF16), vb) * (1.0 / denom)
                o = oh if o is None else o + oh
            attn_scr[r * WINDOW:(r + 1) * WINDOW, m * LANES:(m + 1) * LANES] = o.astype(_BF16)

    g2_ref[0] = gate(2).astype(_BF16)

    y_attn = _dot(attn_scr[...], wao_ref[...])
    part_ref[0] = (g0_scr[...] * y_attn + acc_scr[...]).astype(_BF16)

    kv_scr[:, 0:WINDOW, :] = kv_scr[:, tm:tm + WINDOW, :]
    z_scr[0:SUBLANES, :] = z_scr[tm:tm + SUBLANES, :]


def _mixer_in(x, sinks, nw, win, bg, cos_t, sa_t, sb_t, cw, wao, wco, layer, cast_weights):
    b, l, _ = x.shape
    tm = TOKEN_TILE
    nchunk = tm // SSM_CHUNK
    grid = (b, l // tm)
    in_specs = [
        pl.BlockSpec(memory_space=pltpu.SMEM),
        pl.BlockSpec((1, tm, D_MODEL), lambda bi, li: (bi, li, 0)),
        _layer_spec((1, D_MODEL), layer),
        _const_spec((D_MODEL, IN_COLS)),
        _layer_spec((1, GATE_WIDTH), layer),
        pl.BlockSpec((tm, LANES), lambda bi, li: (li, 0)),
        pl.BlockSpec((tm, LANES), lambda bi, li: (li, 0)),
        pl.BlockSpec((tm, LANES), lambda bi, li: (li, 0)),
        _layer_spec((CONV_K, CONV_WIDTH), layer),
        _const_spec((ATTN_WIDTH, D_MODEL)),
        _const_spec((CONV_WIDTH, D_MODEL)),
    ]
    out_specs = [
        pl.BlockSpec((1, tm, D_MODEL), lambda bi, li: (bi, li, 0)),
        pl.BlockSpec((1, tm, D_MODEL), lambda bi, li: (bi, li, 0)),
        _ssm_io_spec(nchunk),
    ]
    out_shape = [
        jax.ShapeDtypeStruct((b, l, D_MODEL), _BF16),
        jax.ShapeDtypeStruct((b, l, D_MODEL), _BF16),
        jax.ShapeDtypeStruct((SSM_SLABS, l // SSM_CHUNK, b, SSM_CHUNK, LANES), _F32),
    ]
    scratch = [
        pltpu.VMEM((8, WINDOW + tm, LANES), _BF16),
        pltpu.VMEM((SUBLANES + tm, CONV_WIDTH), _F32),
        pltpu.VMEM((tm, ATTN_WIDTH), _BF16),
        pltpu.VMEM((tm, CONV_WIDTH), _BF16),
        pltpu.VMEM((tm, D_MODEL), _F32),
        pltpu.VMEM((tm, D_MODEL), _F32),
    ]
    cast_in, cast_out, cast_shape = _cast_plan(cast_weights, layer, grid)
    return pl.pallas_call(
        _with_casts(_mixer_in_kernel, len(in_specs), len(out_specs), len(cast_weights)),
        grid=grid, in_specs=in_specs + cast_in, out_specs=out_specs + cast_out,
        out_shape=out_shape + cast_shape, scratch_shapes=scratch, name="mixer_in",
        compiler_params=pltpu.CompilerParams(
            dimension_semantics=("arbitrary", "arbitrary"), vmem_limit_bytes=VMEM_LIMIT),
    )(sinks, x, nw, win, bg, cos_t, sa_t, sb_t, cw, wao, wco, *cast_weights)


def _ssm_kernel(u_ref, wc_ref, cct_ref, kt_ref, d_ref, ar_ref, ai_ref, y_ref,
                mb_scr, mt_scr, mct_scr, ucat_scr, s_scr, xr_scr, xi_scr, *, batch):
    t = SSM_CHUNK
    rows = u_ref.shape[1] // t

    @pl.when(pl.program_id(1) == 0)
    def _():
        chan_bits, state_bits = SSM_GROUP.bit_length() - 1, SSM_STATE.bit_length() - 1
        r_grp = lax.broadcasted_iota(jnp.int32, (LANES, 2 * SLAB_STATE), 0) >> chan_bits
        c_grp = (lax.broadcasted_iota(jnp.int32, (LANES, 2 * SLAB_STATE), 1) & (SLAB_STATE - 1)) >> state_bits
        same_state = r_grp == c_grp

        def expand(tbl):
            return jnp.where(same_state, jnp.tile(tbl, (SLAB_GROUPS, 1)), jnp.zeros((), tbl.dtype))

        for st in range(t):
            rs = slice(st * LANES, (st + 1) * LANES)
            mb_scr[rs, :] = expand(wc_ref[0, st]).astype(_BF16)
            mct_scr[rs, :] = expand(cct_ref[0, st]).astype(_BF16)

        lane_grp = lax.broadcasted_iota(jnp.int32, (SSM_GROUP, LANES), 1) >> chan_bits
        on_diag = (lax.broadcasted_iota(jnp.int32, (LANES, LANES), 0)
                   == lax.broadcasted_iota(jnp.int32, (LANES, LANES), 1))
        zero_blk = jnp.zeros((LANES, LANES), _BF16)
        blks = []
        for lag in range(t):
            pieces = []
            for gl in range(SLAB_GROUPS):
                shift = ((gl - lag) * SSM_GROUP) % LANES
                piece = kt_ref[0, gl]
                if shift:
                    piece = pltpu.roll(piece, shift, axis=1)
                pieces.append(jnp.where(lane_grp == gl, piece, 0.0))
            blk = jnp.concatenate(pieces, axis=0)
            if lag == 0:
                blk = blk + jnp.where(on_diag, d_ref[0], 0.0)
            blks.append(blk.astype(_BF16))
        for s_in in range(t):
            for s_out in range(t):
                mt_scr[s_in * LANES:(s_in + 1) * LANES, s_out * LANES:(s_out + 1) * LANES] = (
                    blks[s_out - s_in] if s_out >= s_in else zero_blk)
        xr_scr[...] = jnp.zeros_like(xr_scr)
        xi_scr[...] = jnp.zeros_like(xi_scr)

    for st in range(t):
        ucat_scr[:, st * LANES:(st + 1) * LANES] = u_ref[0, pl.ds(st, rows, stride=t), :].astype(_BF16)

    s_scr[...] = _dot(ucat_scr[...], mb_scr[...])

    ar = jnp.broadcast_to(ar_ref[0], (batch, SLAB_STATE))
    ai = jnp.broadcast_to(ai_ref[0], (batch, SLAB_STATE))

    xr, xi = xr_scr[...], xi_scr[...]
    for c in range(rows // batch):
        rs = slice(c * batch, (c + 1) * batch)
        sr = s_scr[rs, 0:SLAB_STATE]
        si = s_scr[rs, SLAB_STATE:2 * SLAB_STATE]
        s_scr[rs, 0:SLAB_STATE] = xr
        s_scr[rs, SLAB_STATE:2 * SLAB_STATE] = xi
        xr, xi = ar * xr - ai * xi + sr, ar * xi + ai * xr + si
    xr_scr[...] = xr
    xi_scr[...] = xi

    pair = 2 * LANES
    toeplitz = jnp.concatenate(
        [_dot(ucat_scr[:, 0:c0 + pair], mt_scr[0:c0 + pair, c0:c0 + pair]) for c0 in range(0, t * LANES, pair)],
        axis=1)
    y = toeplitz + lax.dot_general(
        s_scr[...].astype(_BF16), mct_scr[...], (((1,), (1,)), ((), ())), preferred_element_type=_F32)
    for st in range(t):
        y_ref[0, pl.ds(st, rows, stride=t), :] = y[:, st * LANES:(st + 1) * LANES]


def _ssm(u, wc, cct, kt, dd, ar, ai, layer, batch):
    nslab, tok_rows, _ = u.shape
    blk = SSM_CHUNK_BLOCK * batch * SSM_CHUNK
    rows = SSM_CHUNK_BLOCK * batch
    width = SSM_CHUNK * LANES
    io = pl.BlockSpec((1, blk, LANES), lambda q, ci: (q, ci, 0))
    par = lambda *shape: pl.BlockSpec((None, 1) + shape, lambda q, ci: (layer, q) + (0,) * len(shape))
    return pl.pallas_call(
        functools.partial(_ssm_kernel, batch=batch),
        grid=(nslab, tok_rows // blk),
        in_specs=[io, par(SSM_CHUNK, SSM_GROUP, 2 * SLAB_STATE), par(SSM_CHUNK, SSM_GROUP, 2 * SLAB_STATE),
                  par(SLAB_GROUPS, SSM_GROUP, SSM_CHUNK * SSM_GROUP), par(1, LANES),
                  par(1, SLAB_STATE), par(1, SLAB_STATE)],
        out_specs=io,
        out_shape=jax.ShapeDtypeStruct(u.shape, _F32),
        scratch_shapes=[
            pltpu.VMEM((width, 2 * SLAB_STATE), _BF16),
            pltpu.VMEM((width, width), _BF16),
            pltpu.VMEM((width, 2 * SLAB_STATE), _BF16),
            pltpu.VMEM((rows, width), _BF16),
            pltpu.VMEM((rows, 2 * SLAB_STATE), _F32),
            pltpu.VMEM((batch, SLAB_STATE), _F32),
            pltpu.VMEM((batch, SLAB_STATE), _F32),
        ],
        name="ssm",
        compiler_params=pltpu.CompilerParams(
            dimension_semantics=("arbitrary", "arbitrary"), vmem_limit_bytes=VMEM_LIMIT),
    )(u, wc, cct, kt, dd, ar, ai)


def _ssm_operators(a_re, a_im, b_re, b_im, c_re, c_im, d, log_dt):
    t = SSM_CHUNK
    p, hh, nq, ng = SSM_STATE, SSM_GROUP, SSM_SLABS, SLAB_GROUPS
    depth = a_re.shape[0]
    hi = lax.Precision.HIGHEST
    a_re, a_im = a_re.astype(_F32), a_im.astype(_F32)
    dt = jnp.exp(log_dt.astype(_F32))[:, :, None]
    b_re, b_im = b_re.astype(_F32), b_im.astype(_F32)
    c_re, c_im = c_re.astype(_F32), c_im.astype(_F32)

    def powers(ar_, ai_, dt_, n):
        mag, ang = jnp.exp(ar_ * dt_ * n), ai_ * dt_ * n
        return mag * jnp.cos(ang), mag * jnp.sin(ang)

    def zoh(ar_, ai_, dt_):
        lr, li_ = powers(ar_, ai_, dt_, 1.0)
        nr, den = lr - 1.0, ar_ * ar_ + ai_ * ai_
        return (nr * ar_ + li_ * ai_) / den, (li_ * ar_ - nr * ai_) / den

    steps = jnp.arange(t, dtype=_F32)[:, None, None]
    pr, pi = powers(a_re[:, :, None, None], a_im[:, :, None, None], dt[:, :, None, None], steps)
    cr, ci = c_re[:, :, None], c_im[:, :, None]
    lhs = jnp.concatenate([cr * pr - ci * pi, -(cr * pi + ci * pr)], axis=-1)
    f_re, f_im = zoh(a_re, a_im, dt)
    rhs = jnp.concatenate([f_re[..., None] * b_re - f_im[..., None] * b_im,
                           f_re[..., None] * b_im + f_im[..., None] * b_re], axis=-2)
    kt = jnp.einsum("ygck,ygmc->ygkm", rhs, lhs.reshape(depth, -1, t * hh, 2 * p), precision=hi)
    kt = kt.reshape(depth, nq, ng, hh, t * hh)
    dd = d.astype(_F32).reshape(depth, nq, 1, LANES)

    lane = lambda v: v.reshape(depth, nq, 1, ng * p)
    la_re, la_im = lane(a_re), lane(a_im)
    ldt = lane(jnp.broadcast_to(dt, a_re.shape))
    lf_re, lf_im = zoh(la_re, la_im, ldt)
    to_rows = lambda v, axes: v.reshape(depth, nq, ng, *v.shape[2:]).transpose(axes).reshape(depth, nq, hh, ng * p)
    lb_re, lb_im = to_rows(b_re, (0, 1, 4, 2, 3)), to_rows(b_im, (0, 1, 4, 2, 3))
    lc_re, lc_im = to_rows(c_re, (0, 1, 3, 2, 4)), to_rows(c_im, (0, 1, 3, 2, 4))
    bb_re, bb_im = lf_re * lb_re - lf_im * lb_im, lf_re * lb_im + lf_im * lb_re

    rev = (t - 1 - jnp.arange(t, dtype=_F32))[:, None, None]
    rr, ri = powers(la_re[:, :, None], la_im[:, :, None], ldt[:, :, None], rev)
    wc = jnp.concatenate([rr * bb_re[:, :, None] - ri * bb_im[:, :, None],
                          rr * bb_im[:, :, None] + ri * bb_re[:, :, None]], axis=-1)
    fwd = (1.0 + jnp.arange(t, dtype=_F32))[:, None, None]
    fr, fi = powers(la_re[:, :, None], la_im[:, :, None], ldt[:, :, None], fwd)
    cct = jnp.concatenate([lc_re[:, :, None] * fr - lc_im[:, :, None] * fi,
                           -(lc_re[:, :, None] * fi + lc_im[:, :, None] * fr)], axis=-1)
    ar, ai = powers(la_re, la_im, ldt, float(t))
    return wc.astype(_BF16), cct.astype(_BF16), kt, dd, ar, ai


def _mixer_out_kernel(x_ref, part_ref, g2_ref, y_ref, wglu_ref, wso_ref, wmix_ref, nf_ref,
                      wfi_ref, wfo_ref, nlast_ref, out_ref, *, final):
    tm = x_ref.shape[1]
    y = jnp.concatenate([y_ref[sl].reshape(tm, LANES) for sl in range(SSM_SLABS)], axis=1)
    ys = 0.5 * y * (1.0 + jnp.tanh(math.sqrt(2.0 / math.pi) * (y + 0.044715 * (y * y * y))))
    glu = ys * _sigmoid(_dot(ys.astype(_BF16), wglu_ref[...]))
    y_ssm = _dot(glu.astype(_BF16), wso_ref[...])
    merged = part_ref[0].astype(_F32) + g2_ref[0].astype(_F32) * y_ssm
    x1 = x_ref[0] + _dot(merged.astype(_BF16), wmix_ref[...])

    h = _rmsnorm(x1, nf_ref[...]).astype(_BF16)
    gt = _dot(h, wfi_ref[:, 0:FFN_HIDDEN])
    up = _dot(h, wfi_ref[:, FFN_HIDDEN:2 * FFN_HIDDEN])
    act = (gt * _sigmoid(gt) * up).astype(_BF16)
    x2 = x1 + _dot(act, wfo_ref[...])
    if final:
        x2 = _rmsnorm(x2, nlast_ref[...])
    out_ref[0] = x2


def _mixer_out(x, part, g2, y, wglu, wso, wmix, nf, wfi, wfo, nlast, layer, final, cast_weights):
    b, l, _ = x.shape
    tm = TOKEN_TILE
    nchunk = tm // SSM_CHUNK
    grid = (b, l // tm)
    tok = pl.BlockSpec((1, tm, D_MODEL), lambda bi, li: (bi, li, 0))
    in_specs = [
        tok, tok, tok,
        _ssm_io_spec(nchunk),
        _const_spec((SSM_WIDTH, SSM_WIDTH)),
        _const_spec((SSM_WIDTH, D_MODEL)),
        _const_spec((D_MODEL, D_MODEL)),
        _layer_spec((1, D_MODEL), layer),
        _const_spec((D_MODEL, 2 * FFN_HIDDEN)),
        _const_spec((FFN_HIDDEN, D_MODEL)),
        _const_spec((1, D_MODEL)),
    ]
    cast_in, cast_out, cast_shape = _cast_plan(cast_weights, layer + 1, grid)
    body = functools.partial(_mixer_out_kernel, final=final)
    return pl.pallas_call(
        _with_casts(body, len(in_specs), 1, len(cast_weights)),
        grid=grid, in_specs=in_specs + cast_in, out_specs=[tok] + cast_out,
        out_shape=[jax.ShapeDtypeStruct((b, l, D_MODEL), _F32)] + cast_shape,
        name="mixer_out",
        compiler_params=pltpu.CompilerParams(
            dimension_semantics=("arbitrary", "arbitrary"), vmem_limit_bytes=VMEM_LIMIT),
    )(x, part, g2, y, wglu, wso, wmix, nf, wfi, wfo, nlast, *cast_weights)


def _rope_tables(seq_len):
    half = ROT_DIM // 2
    pos = jnp.arange(seq_len, dtype=_F32)
    inv_freq = ROPE_THETA ** (-jnp.arange(0, ROT_DIM, 2, dtype=_F32) / ROT_DIM)
    ang = pos[:, None] * inv_freq[None, :]
    cos, sin = jnp.cos(ang), jnp.sin(ang)
    dim = np.arange(LANES) % HEAD_DIM
    freq = np.arange(half)[:, None]
    first, second = dim < half, (dim >= half) & (dim < ROT_DIM)
    pick_cos = ((dim % half == freq) & (dim < ROT_DIM)).astype(np.float32)
    pick_sa = -((dim == freq) & first).astype(np.float32)
    pick_sb = ((dim - half == freq) & second).astype(np.float32)
    spread = lambda v, m: jnp.dot(v, jnp.asarray(m), precision=lax.Precision.HIGHEST)
    identity = jnp.asarray((dim >= ROT_DIM).astype(np.float32))
    return spread(cos, pick_cos) + identity, spread(sin, pick_sa), spread(sin, pick_sb)


def kernel(x, norm_mix, w_in, b_gate, attn_sinks, w_attn_o, conv_w, w_conv_o, ssm_a_re, ssm_a_im,
           ssm_b_re, ssm_b_im, ssm_c_re, ssm_c_im, ssm_d, ssm_log_dt, w_ssm_glu, w_ssm_o, w_mix_o,
           norm_ffn, w_ffn_in, w_ffn_out, norm_final):
    b, l, d_model = x.shape
    depth = w_in.shape[0]
    assert d_model == D_MODEL and l % TOKEN_TILE == 0 and b % SUBLANES == 0
    assert l % (SSM_CHUNK * SSM_CHUNK_BLOCK) == 0
    assert w_in.shape[2] == IN_COLS and w_ffn_out.shape[1] == FFN_HIDDEN
    cos_t, sa_t, sb_t = _rope_tables(l)
    f32 = lambda p: p.astype(_F32)
    norm_mix, norm_ffn = f32(norm_mix)[:, None, :], f32(norm_ffn)[:, None, :]
    b_gate, conv_w, attn_sinks = f32(b_gate)[:, None, :], f32(conv_w), f32(attn_sinks)
    nlast = f32(norm_final).reshape(1, D_MODEL)
    wc, cct, kt, dd, ar, ai = _ssm_operators(
        ssm_a_re, ssm_a_im, ssm_b_re, ssm_b_im, ssm_c_re, ssm_c_im, ssm_d, ssm_log_dt)
    in_weights = tuple(f32(w) for w in (w_in, w_attn_o, w_conv_o))
    out_weights = tuple(f32(w) for w in (w_ssm_glu, w_ssm_o, w_mix_o, w_ffn_in, w_ffn_out))
    w_mixer_in = [w[0].astype(_BF16) for w in in_weights]
    x = f32(x)
    for i in range(depth):
        last = i == depth - 1
        win, wao, wco = w_mixer_in
        part, g2, u, *w_mixer_out = _mixer_in(
            x, attn_sinks[i], norm_mix, win, b_gate, cos_t, sa_t, sb_t, conv_w, wao, wco, i, out_weights)
        y = _ssm(u.reshape(SSM_SLABS, l * b, LANES), wc, cct, kt, dd, ar, ai, i, b).reshape(u.shape)
        x, *w_mixer_in = _mixer_out(
            x, part, g2, y, *w_mixer_out[:3], norm_ffn, *w_mixer_out[3:], nlast, i, last,
            () if last else in_weights)
    return x
```

```python
import functools
import math

import jax
import jax.numpy as jnp
from jax import lax
from jax.experimental import pallas as pl
from jax.experimental.pallas import tpu as pltpu

D_MODEL = 1024
N_Q_HEADS = 8
N_KV_HEADS = 2
HEAD_DIM = 64
Q_GROUP = N_Q_HEADS // N_KV_HEADS
WINDOW = 128
ROPE_THETA = 500000.0
ROT_DIM = HEAD_DIM // 4
ATTN_WIDTH = N_Q_HEADS * HEAD_DIM
KV_WIDTH = N_KV_HEADS * HEAD_DIM
NEG_INF = -1e30
CONV_WIDTH = D_MODEL // 2
CONV_K = 3
SSM_WIDTH = D_MODEL // 2
SSM_GROUP = 16
SSM_GROUPS = SSM_WIDTH // SSM_GROUP
SSM_STATE = 64
GATE_WIDTH = 3 * D_MODEL
FFN_HIDDEN = 2816
NORM_EPS = 1e-6
LOG2_E = math.log2(math.e)

_Q0 = 0
_KV0 = _Q0 + ATTN_WIDTH
_CB0 = _KV0 + 2 * KV_WIDTH
_CC0 = _CB0 + CONV_WIDTH
_CX0 = _CC0 + CONV_WIDTH
_U0 = _CX0 + CONV_WIDTH
_G0 = _U0 + SSM_WIDTH
IN_COLS = _G0 + GATE_WIDTH

LANES = 128
SUBLANES = 8
BF16_ROWS = 2 * SUBLANES
SSM_CHUNK = 8
SSM_SLABS = SSM_WIDTH // LANES
SLAB_GROUPS = LANES // SSM_GROUP
SLAB_STATE = SLAB_GROUPS * SSM_STATE
TOKEN_TILE = 512
SSM_CHUNK_BLOCK = 128
MATMUL_ROWS = 256
VMEM_LIMIT = 56 * 1024 * 1024

_BF16 = jnp.bfloat16
_F32 = jnp.float32


def _dot(a, b):
    m = a.shape[0]
    if m <= MATMUL_ROWS:
        return jnp.dot(a, b, preferred_element_type=_F32)
    return jnp.concatenate(
        [jnp.dot(a[i:i + MATMUL_ROWS], b, preferred_element_type=_F32) for i in range(0, m, MATMUL_ROWS)], axis=0)


def _sigmoid(x):
    return 0.5 * jnp.tanh(0.5 * x) + 0.5


def _rmsnorm(x, w):
    ms = jnp.mean(x * x, axis=-1, keepdims=True)
    return x * lax.rsqrt(ms + NORM_EPS) * w


def _const_spec(shape):
    return pl.BlockSpec(shape, lambda *_: (0,) * len(shape), pipeline_mode=pl.Buffered(1))


def _layer_spec(shape, layer):
    return pl.BlockSpec((None,) + shape, lambda *_: (layer,) + (0,) * len(shape), pipeline_mode=pl.Buffered(1))


def _cast_plan(weights, layer, grid):
    nsteps = grid[0] * grid[1]
    in_specs, out_specs, out_shapes = [], [], []
    for w in weights:
        _, rows, cols = w.shape
        ncol = next(n for n in (1, 2, 4, 8)
                    if (rows * n) % (nsteps * BF16_ROWS) == 0 and cols % (n * LANES) == 0)
        nrow = nsteps // ncol
        block = (rows // nrow, cols // ncol)

        def index(bi, li, nrow=nrow):
            step = bi * grid[1] + li
            return step % nrow, step // nrow

        in_specs.append(pl.BlockSpec((None,) + block, lambda bi, li, index=index: (layer,) + index(bi, li)))
        out_specs.append(pl.BlockSpec(block, index))
        out_shapes.append(jax.ShapeDtypeStruct((rows, cols), _BF16))
    return in_specs, out_specs, out_shapes


def _with_casts(body, n_in, n_out, n_cast):
    def kernel(*refs):
        ins, rest = refs[:n_in], refs[n_in:]
        cast_src, rest = rest[:n_cast], rest[n_cast:]
        outs, rest = rest[:n_out], rest[n_out:]
        cast_dst, scratch = rest[:n_cast], rest[n_cast:]
        body(*ins, *outs, *scratch)
        for src, dst in zip(cast_src, cast_dst):
            dst[...] = src[...].astype(_BF16)
    return kernel


def _ssm_io_spec(nchunk):
    return pl.BlockSpec((SSM_SLABS, nchunk, None, SSM_CHUNK, LANES), lambda bi, li: (0, li, bi, 0, 0))


def _mixer_in_kernel(sinks_ref, x_ref, nw_ref, win_ref, bg_ref, cos_ref, sa_ref, sb_ref, cw_ref,
                     wao_ref, wco_ref, part_ref, g2_ref, u_ref,
                     kv_scr, z_scr, attn_scr, yc_scr, acc_scr, g0_scr):
    tm = x_ref.shape[1]
    li = pl.program_id(1)

    @pl.when(li == 0)
    def _():
        kv_scr[:, 0:WINDOW, :] = jnp.zeros((8, WINDOW, LANES), _BF16)
        z_scr[0:SUBLANES, :] = jnp.zeros((SUBLANES, CONV_WIDTH), _F32)

    x = x_ref[0]
    h = _rmsnorm(x, nw_ref[...]).astype(_BF16)

    cos = cos_ref[...]
    sa = sa_ref[...]
    sb = sb_ref[...]

    def rope(t):
        return (t * cos + pltpu.roll(t, LANES - ROT_DIM // 2, axis=1) * sa
                + pltpu.roll(t, ROT_DIM // 2, axis=1) * sb)

    kv = _dot(h, win_ref[:, _KV0:_KV0 + 2 * KV_WIDTH])
    lane = lax.broadcasted_iota(jnp.int32, (tm, LANES), 1)
    lo = lane < HEAD_DIM
    for kind in range(2):
        t = kv[:, kind * LANES:(kind + 1) * LANES]
        if kind == 0:
            t = rope(t)
        t_sw = pltpu.roll(t, HEAD_DIM, axis=1)
        zero = jnp.zeros_like(t)
        variants = (jnp.where(lo, t, zero), jnp.where(lo, zero, t_sw),
                    jnp.where(lo, t_sw, zero), jnp.where(lo, zero, t))
        for n, val in enumerate(variants):
            kv_scr[kind * 4 + n, WINDOW:WINDOW + tm, :] = val.astype(_BF16)

    q = _dot(h, win_ref[:, _Q0:_Q0 + ATTN_WIDTH])
    row = lax.broadcasted_iota(jnp.int32, (WINDOW, 2 * WINDOW), 0)
    col = lax.broadcasted_iota(jnp.int32, (WINDOW, 2 * WINDOW), 1)
    band_ok = (col > row) & (col <= row + WINDOW)
    first_ok = band_ok & (col >= jnp.where(li > 0, 0, WINDOW))

    def gate(n):
        c0 = _G0 + n * D_MODEL
        return _sigmoid(_dot(h, win_ref[:, c0:c0 + D_MODEL]) + bg_ref[:, n * D_MODEL:(n + 1) * D_MODEL])


    cb = _dot(h, win_ref[:, _CB0:_CB0 + CONV_WIDTH])
    cc = _dot(h, win_ref[:, _CC0:_CC0 + CONV_WIDTH])
    cx = _dot(h, win_ref[:, _CX0:_CX0 + CONV_WIDTH])
    z = cc * cx
    z_scr[SUBLANES:SUBLANES + tm, :] = z
    z1 = z_scr[SUBLANES - 1:SUBLANES - 1 + tm, :]
    z2 = z_scr[SUBLANES - 2:SUBLANES - 2 + tm, :]
    conv = cw_ref[0:1, :] * z2 + cw_ref[1:2, :] * z1 + cw_ref[2:3, :] * z
    yc_scr[...] = (cb * conv).astype(_BF16)
    acc_scr[...] = gate(1) * _dot(yc_scr[...], wco_ref[...])

    g0_scr[...] = gate(0)
    u = _dot(h, win_ref[:, _U0:_U0 + SSM_WIDTH])
    nchunk = tm // SSM_CHUNK
    for sl in range(SSM_SLABS):
        u_ref[sl] = u[:, sl * LANES:(sl + 1) * LANES].reshape(nchunk, SSM_CHUNK, LANES)

    for m in range(N_Q_HEADS // 2):
        qp = (rope(q[:, m * LANES:(m + 1) * LANES]) * (HEAD_DIM ** -0.5 * LOG2_E)).astype(_BF16)
        j = (2 * m) // Q_GROUP
        for r in range(tm // WINDOW):
            qb = qp[r * WINDOW:(r + 1) * WINDOW]
            mask = first_ok if r == 0 else band_ok
            o = None
            for half in range(2):
                sink = sinks_ref[2 * m + half] * LOG2_E
                kb = kv_scr[j * 2 + half, r * WINDOW:(r + 2) * WINDOW, :]
                vb = kv_scr[4 + j * 2 + half, r * WINDOW:(r + 2) * WINDOW, :]
                s = lax.dot_general(qb, kb, (((1,), (1,)), ((), ())), preferred_element_type=_F32)
                s = jnp.where(mask, s, NEG_INF)
                mx = jnp.maximum(jnp.max(s, axis=-1, keepdims=True), sink)
                p = jnp.exp2(s - mx)
                denom = jnp.sum(p, axis=-1, keepdims=True) + jnp.exp2(sink - mx)
                oh = _dot(p.astype(_BF16), vb) * (1.0 / denom)
                o = oh if o is None else o + oh
            attn_scr[r * WINDOW:(r + 1) * WINDOW, m * LANES:(m + 1) * LANES] = o.astype(_BF16)

    g2_ref[0] = gate(2).astype(_BF16)

    y_attn = _dot(attn_scr[...], wao_ref[...])
    part_ref[0] = (g0_scr[...] * y_attn + acc_scr[...]).astype(_BF16)

    kv_scr[:, 0:WINDOW, :] = kv_scr[:, tm:tm + WINDOW, :]
    z_scr[0:SUBLANES, :] = z_scr[tm:tm + SUBLANES, :]


def _mixer_in(x, sinks, nw, win, bg, cos_t, sa_t, sb_t, cw, wao, wco, layer, cast_weights):
    b, l, _ = x.shape
    tm = TOKEN_TILE
    nchunk = tm // SSM_CHUNK
    grid = (b, l // tm)
    in_specs = [
        pl.BlockSpec(memory_space=pltpu.SMEM),
        pl.BlockSpec((1, tm, D_MODEL), lambda bi, li: (bi, li, 0)),
        _layer_spec((1, D_MODEL), layer),
        _const_spec((D_MODEL, IN_COLS)),
        _layer_spec((1, GATE_WIDTH), layer),
        pl.BlockSpec((tm, LANES), lambda bi, li: (li, 0)),
        pl.BlockSpec((tm, LANES), lambda bi, li: (li, 0)),
        pl.BlockSpec((tm, LANES), lambda bi, li: (li, 0)),
        _layer_spec((CONV_K, CONV_WIDTH), layer),
        _const_spec((ATTN_WIDTH, D_MODEL)),
        _const_spec((CONV_WIDTH, D_MODEL)),
    ]
    out_specs = [
        pl.BlockSpec((1, tm, D_MODEL), lambda bi, li: (bi, li, 0)),
        pl.BlockSpec((1, tm, D_MODEL), lambda bi, li: (bi, li, 0)),
        _ssm_io_spec(nchunk),
    ]
    out_shape = [
        jax.ShapeDtypeStruct((b, l, D_MODEL), _BF16),
        jax.ShapeDtypeStruct((b, l, D_MODEL), _BF16),
        jax.ShapeDtypeStruct((SSM_SLABS, l // SSM_CHUNK, b, SSM_CHUNK, LANES), _F32),
    ]
    scratch = [
        pltpu.VMEM((8, WINDOW + tm, LANES), _BF16),
        pltpu.VMEM((SUBLANES + tm, CONV_WIDTH), _F32),
        pltpu.VMEM((tm, ATTN_WIDTH), _BF16),
        pltpu.VMEM((tm, CONV_WIDTH), _BF16),
        pltpu.VMEM((tm, D_MODEL), _F32),
        pltpu.VMEM((tm, D_MODEL), _F32),
    ]
    cast_in, cast_out, cast_shape = _cast_plan(cast_weights, layer, grid)
    return pl.pallas_call(
        _with_casts(_mixer_in_kernel, len(in_specs), len(out_specs), len(cast_weights)),
        grid=grid, in_specs=in_specs + cast_in, out_specs=out_specs + cast_out,
        out_shape=out_shape + cast_shape, scratch_shapes=scratch, name="mixer_in",
        compiler_params=pltpu.CompilerParams(
            dimension_semantics=("arbitrary", "arbitrary"), vmem_limit_bytes=VMEM_LIMIT),
    )(sinks, x, nw, win, bg, cos_t, sa_t, sb_t, cw, wao, wco, *cast_weights)


def _ssm_kernel(u_ref, wc_ref, cct_ref, kt_ref, d_ref, ar_ref, ai_ref, y_ref,
                mb_scr, mt_scr, mct_scr, ucat_scr, s_scr, xr_scr, xi_scr, *, batch):
    t = SSM_CHUNK
    rows = u_ref.shape[1] // t

    @pl.when(pl.program_id(1) == 0)
    def _():
        chan_bits, state_bits = SSM_GROUP.bit_length() - 1, SSM_STATE.bit_length() - 1
        r_grp = lax.broadcasted_iota(jnp.int32, (LANES, 2 * SLAB_STATE), 0) >> chan_bits
        c_grp = (lax.broadcasted_iota(jnp.int32, (LANES, 2 * SLAB_STATE), 1) & (SLAB_STATE - 1)) >> state_bits
        same_state = r_grp == c_grp

        def expand(tbl):
            return jnp.where(same_state, jnp.tile(tbl, (SLAB_GROUPS, 1)), jnp.zeros((), tbl.dtype))

        for st in range(t):
            rs = slice(st * LANES, (st + 1) * LANES)
            mb_scr[rs, :] = expand(wc_ref[0, st]).astype(_BF16)
            mct_scr[rs, :] = expand(cct_ref[0, st]).astype(_BF16)

        lane_grp = lax.broadcasted_iota(jnp.int32, (SSM_GROUP, LANES), 1) >> chan_bits
        on_diag = (lax.broadcasted_iota(jnp.int32, (LANES, LANES), 0)
                   == lax.broadcasted_iota(jnp.int32, (LANES, LANES), 1))
        zero_blk = jnp.zeros((LANES, LANES), _BF16)
        blks = []
        for lag in range(t):
            pieces = []
            for gl in range(SLAB_GROUPS):
                shift = ((gl - lag) * SSM_GROUP) % LANES
                piece = kt_ref[0, gl]
                if shift:
                    piece = pltpu.roll(piece, shift, axis=1)
                pieces.append(jnp.where(lane_grp == gl, piece, 0.0))
            blk = jnp.concatenate(pieces, axis=0)
            if lag == 0:
                blk = blk + jnp.where(on_diag, d_ref[0], 0.0)
            blks.append(blk.astype(_BF16))
        for s_in in range(t):
            for s_out in range(t):
                mt_scr[s_in * LANES:(s_in + 1) * LANES, s_out * LANES:(s_out + 1) * LANES] = (
                    blks[s_out - s_in] if s_out >= s_in else zero_blk)
        xr_scr[...] = jnp.zeros_like(xr_scr)
        xi_scr[...] = jnp.zeros_like(xi_scr)

    for st in range(t):
        ucat_scr[:, st * LANES:(st + 1) * LANES] = u_ref[0, pl.ds(st, rows, stride=t), :].astype(_BF16)

    s_scr[...] = _dot(ucat_scr[...], mb_scr[...])

    ar = jnp.broadcast_to(ar_ref[0], (batch, SLAB_STATE))
    ai = jnp.broadcast_to(ai_ref[0], (batch, SLAB_STATE))

    xr, xi = xr_scr[...], xi_scr[...]
    for c in range(rows // batch):
        rs = slice(c * batch, (c + 1) * batch)
        sr = s_scr[rs, 0:SLAB_STATE]
        si = s_scr[rs, SLAB_STATE:2 * SLAB_STATE]
        s_scr[rs, 0:SLAB_STATE] = xr
        s_scr[rs, SLAB_STATE:2 * SLAB_STATE] = xi
        xr, xi = ar * xr - ai * xi + sr, ar * xi + ai * xr + si
    xr_scr[...] = xr
    xi_scr[...] = xi

    pair = 2 * LANES
    toeplitz = jnp.concatenate(
        [_dot(ucat_scr[:, 0:c0 + pair], mt_scr[0:c0 + pair, c0:c0 + pair]) for c0 in range(0, t * LANES, pair)],
        axis=1)
    y = toeplitz + lax.dot_general(
        s_scr[...].astype(_BF16), mct_scr[...], (((1,), (1,)), ((), ())), preferred_element_type=_F32)
    for st in range(t):
        y_ref[0, pl.ds(st, rows, stride=t), :] = y[:, st * LANES:(st + 1) * LANES]


def _ssm(u, wc, cct, kt, dd, ar, ai, layer, batch):
    nslab, tok_rows, _ = u.shape
    blk = SSM_CHUNK_BLOCK * batch * SSM_CHUNK
    rows = SSM_CHUNK_BLOCK * batch
    width = SSM_CHUNK * LANES
    io = pl.BlockSpec((1, blk, LANES), lambda q, ci: (q, ci, 0))
    par = lambda *shape: pl.BlockSpec((None, 1) + shape, lambda q, ci: (layer, q) + (0,) * len(shape))
    return pl.pallas_call(
        functools.partial(_ssm_kernel, batch=batch),
        grid=(nslab, tok_rows // blk),
        in_specs=[io, par(SSM_CHUNK, SSM_GROUP, 2 * SLAB_STATE), par(SSM_CHUNK, SSM_GROUP, 2 * SLAB_STATE),
                  par(SLAB_GROUPS, SSM_GROUP, SSM_CHUNK * SSM_GROUP), par(1, LANES),
                  par(1, SLAB_STATE), par(1, SLAB_STATE)],
        out_specs=io,
        out_shape=jax.ShapeDtypeStruct(u.shape, _F32),
        scratch_shapes=[
            pltpu.VMEM((width, 2 * SLAB_STATE), _BF16),
            pltpu.VMEM((width, width), _BF16),
            pltpu.VMEM((width, 2 * SLAB_STATE), _BF16),
            pltpu.VMEM((rows, width), _BF16),
            pltpu.VMEM((rows, 2 * SLAB_STATE), _F32),
            pltpu.VMEM((batch, SLAB_STATE), _F32),
            pltpu.VMEM((batch, SLAB_STATE), _F32),
        ],
        name="ssm",
        compiler_params=pltpu.CompilerParams(
            dimension_semantics=("arbitrary", "arbitrary"), vmem_limit_bytes=VMEM_LIMIT),
    )(u, wc, cct, kt, dd, ar, ai)


def _ssm_operators(a_re, a_im, b_re, b_im, c_re, c_im, d, log_dt):
    t = SSM_CHUNK
    p, hh, nq, ng = SSM_STATE, SSM_GROUP, SSM_SLABS, SLAB_GROUPS
    depth = a_re.shape[0]
    hi = lax.Precision.HIGHEST
    a_re, a_im = a_re.astype(_F32), a_im.astype(_F32)
    dt = jnp.exp(log_dt.astype(_F32))[:, :, None]
    b_re, b_im = b_re.astype(_F32), b_im.astype(_F32)
    c_re, c_im = c_re.astype(_F32), c_im.astype(_F32)

    def powers(ar_, ai_, dt_, n):
        mag, ang = jnp.exp(ar_ * dt_ * n), ai_ * dt_ * n
        return mag * jnp.cos(ang), mag * jnp.sin(ang)

    def zoh(ar_, ai_, dt_):
        lr, li_ = powers(ar_, ai_, dt_, 1.0)
        nr, den = lr - 1.0, ar_ * ar_ + ai_ * ai_
        return (nr * ar_ + li_ * ai_) / den, (li_ * ar_ - nr * ai_) / den

    steps = jnp.arange(t, dtype=_F32)[:, None, None]
    pr, pi = powers(a_re[:, :, None, None], a_im[:, :, None, None], dt[:, :, None, None], steps)
    cr, ci = c_re[:, :, None], c_im[:, :, None]
    lhs = jnp.concatenate([cr * pr - ci * pi, -(cr * pi + ci * pr)], axis=-1)
    f_re, f_im = zoh(a_re, a_im, dt)
    rhs = jnp.concatenate([f_re[..., None] * b_re - f_im[..., None] * b_im,
                           f_re[..., None] * b_im + f_im[..., None] * b_re], axis=-2)
    kt = jnp.einsum("ygck,ygmc->ygkm", rhs, lhs.reshape(depth, -1, t * hh, 2 * p), precision=hi)
    kt = kt.reshape(depth, nq, ng, hh, t * hh)
    dd = d.astype(_F32).reshape(depth, nq, 1, LANES)

    lane = lambda v: v.reshape(depth, nq, 1, ng * p)
    la_re, la_im = lane(a_re), lane(a_im)
    ldt = lane(jnp.broadcast_to(dt, a_re.shape))
    lf_re, lf_im = zoh(la_re, la_im, ldt)
    to_rows = lambda v, axes: v.reshape(depth, nq, ng, *v.shape[2:]).transpose(axes).reshape(depth, nq, hh, ng * p)
    lb_re, lb_im = to_rows(b_re, (0, 1, 4, 2, 3)), to_rows(b_im, (0, 1, 4, 2, 3))
    lc_re, lc_im = to_rows(c_re, (0, 1, 3, 2, 4)), to_rows(c_im, (0, 1, 3, 2, 4))
    bb_re, bb_im = lf_re * lb_re - lf_im * lb_im, lf_re * lb_im + lf_im * lb_re

    rev = (t - 1 - jnp.arange(t, dtype=_F32))[:, None, None]
    rr, ri = powers(la_re[:, :, None], la_im[:, :, None], ldt[:, :, None], rev)
    wc = jnp.concatenate([rr * bb_re[:, :, None] - ri * bb_im[:, :, None],
                          rr * bb_im[:, :, None] + ri * bb_re[:, :, None]], axis=-1)
    fwd = (1.0 + jnp.arange(t, dtype=_F32))[:, None, None]
    fr, fi = powers(la_re[:, :, None], la_im[:, :, None], ldt[:, :, None], fwd)
    cct = jnp.concatenate([lc_re[:, :, None] * fr - lc_im[:, :, None] * fi,
                           -(lc_re[:, :, None] * fi + lc_im[:, :, None] * fr)], axis=-1)
    ar, ai = powers(la_re, la_im, ldt, float(t))
    return wc.astype(_BF16), cct.astype(_BF16), kt, dd, ar, ai


def _mixer_out_kernel(x_ref, part_ref, g2_ref, y_ref, wglu_ref, wso_ref, wmix_ref, nf_ref,
                      wfi_ref, wfo_ref, nlast_ref, out_ref, *, final):
    tm = x_ref.shape[1]
    y = jnp.concatenate([y_ref[sl].reshape(tm, LANES) for sl in range(SSM_SLABS)], axis=1)
    ys = 0.5 * y * (1.0 + jnp.tanh(math.sqrt(2.0 / math.pi) * (y + 0.044715 * (y * y * y))))
    glu = ys * _sigmoid(_dot(ys.astype(_BF16), wglu_ref[...]))
    y_ssm = _dot(glu.astype(_BF16), wso_ref[...])
    merged = part_ref[0].astype(_F32) + g2_ref[0].astype(_F32) * y_ssm
    x1 = x_ref[0] + _dot(merged.astype(_BF16), wmix_ref[...])

    h = _rmsnorm(x1, nf_ref[...]).astype(_BF16)
    gt = _dot(h, wfi_ref[:, 0:FFN_HIDDEN])
    up = _dot(h, wfi_ref[:, FFN_HIDDEN:2 * FFN_HIDDEN])
    act = (gt * _sigmoid(gt) * up).astype(_BF16)
    x2 = x1 + _dot(act, wfo_ref[...])
    if final:
        x2 = _rmsnorm(x2, nlast_ref[...])
    out_ref[0] = x2


def _mixer_out(x, part, g2, y, wglu, wso, wmix, nf, wfi, wfo, nlast, layer, final, cast_weights):
    b, l, _ = x.shape
    tm = TOKEN_TILE
    nchunk = tm // SSM_CHUNK
    grid = (b, l // tm)
    tok = pl.BlockSpec((1, tm, D_MODEL), lambda bi, li: (bi, li, 0))
    in_specs = [
        tok, tok, tok,
        _ssm_io_spec(nchunk),
        _const_spec((SSM_WIDTH, SSM_WIDTH)),
        _const_spec((SSM_WIDTH, D_MODEL)),
        _const_spec((D_MODEL, D_MODEL)),
        _layer_spec((1, D_MODEL), layer),
        _const_spec((D_MODEL, 2 * FFN_HIDDEN)),
        _const_spec((FFN_HIDDEN, D_MODEL)),
        _const_spec((1, D_MODEL)),
    ]
    cast_in, cast_out, cast_shape = _cast_plan(cast_weights, layer + 1, grid)
    body = functools.partial(_mixer_out_kernel, final=final)
    return pl.pallas_call(
        _with_casts(body, len(in_specs), 1, len(cast_weights)),
        grid=grid, in_specs=in_specs + cast_in, out_specs=[tok] + cast_out,
        out_shape=[jax.ShapeDtypeStruct((b, l, D_MODEL), _F32)] + cast_shape,
        name="mixer_out",
        compiler_params=pltpu.CompilerParams(
            dimension_semantics=("arbitrary", "arbitrary"), vmem_limit_bytes=VMEM_LIMIT),
    )(x, part, g2, y, wglu, wso, wmix, nf, wfi, wfo, nlast, *cast_weights)


def _rope_tables(seq_len):
    half = ROT_DIM // 2
    pos = jnp.arange(seq_len, dtype=_F32)
    inv_freq = ROPE_THETA ** (-jnp.arange(0, ROT_DIM, 2, dtype=_F32) / ROT_DIM)
    ang = pos[:, None] * inv_freq[None, :]
    cos, sin = jnp.cos(ang), jnp.sin(ang)
    ones = jnp.ones((seq_len, HEAD_DIM - ROT_DIM), _F32)
    zeros_h = jnp.zeros((seq_len, half), _F32)
    zeros_r = jnp.zeros((seq_len, HEAD_DIM - ROT_DIM), _F32)
    cos_h = jnp.concatenate([cos, cos, ones], axis=1)
    sa_h = jnp.concatenate([-sin, zeros_h, zeros_r], axis=1)
    sb_h = jnp.concatenate([zeros_h, sin, zeros_r], axis=1)
    rep = LANES // HEAD_DIM
    return jnp.tile(cos_h, (1, rep)), jnp.tile(sa_h, (1, rep)), jnp.tile(sb_h, (1, rep))


def kernel(x, norm_mix, w_in, b_gate, attn_sinks, w_attn_o, conv_w, w_conv_o, ssm_a_re, ssm_a_im,
           ssm_b_re, ssm_b_im, ssm_c_re, ssm_c_im, ssm_d, ssm_log_dt, w_ssm_glu, w_ssm_o, w_mix_o,
           norm_ffn, w_ffn_in, w_ffn_out, norm_final):
    b, l, d_model = x.shape
    depth = w_in.shape[0]
    assert d_model == D_MODEL and l % TOKEN_TILE == 0 and b % SUBLANES == 0
    assert l % (SSM_CHUNK * SSM_CHUNK_BLOCK) == 0
    assert w_in.shape[2] == IN_COLS and w_ffn_out.shape[1] == FFN_HIDDEN
    cos_t, sa_t, sb_t = _rope_tables(l)
    f32 = lambda p: p.astype(_F32)
    norm_mix, norm_ffn = f32(norm_mix)[:, None, :], f32(norm_ffn)[:, None, :]
    b_gate, conv_w, attn_sinks = f32(b_gate)[:, None, :], f32(conv_w), f32(attn_sinks)
    nlast = f32(norm_final).reshape(1, D_MODEL)
    wc, cct, kt, dd, ar, ai = _ssm_operators(
        ssm_a_re, ssm_a_im, ssm_b_re, ssm_b_im, ssm_c_re, ssm_c_im, ssm_d, ssm_log_dt)
    in_weights = tuple(f32(w) for w in (w_in, w_attn_o, w_conv_o))
    out_weights = tuple(f32(w) for w in (w_ssm_glu, w_ssm_o, w_mix_o, w_ffn_in, w_ffn_out))
    w_mixer_in = [w[0].astype(_BF16) for w in in_weights]
    x = f32(x)
    for i in range(depth):
        last = i == depth - 1
        win, wao, wco = w_mixer_in
        part, g2, u, *w_mixer_out = _mixer_in(
            x, attn_sinks[i], norm_mix, win, b_gate, cos_t, sa_t, sb_t, conv_w, wao, wco, i, out_weights)
        y = _ssm(u.reshape(SSM_SLABS, l * b, LANES), wc, cct, kt, dd, ar, ai, i, b).reshape(u.shape)
        x, *w_mixer_in = _mixer_out(
            x, part, g2, y, *w_mixer_out[:3], norm_ffn, *w_mixer_out[3:], nlast, i, last,
            () if last else in_weights)
    return x
```

```python
import functools
import math

import jax
import jax.numpy as jnp
from jax import lax
from jax.experimental import pallas as pl
from jax.experimental.pallas import tpu as pltpu

D_MODEL = 1024
N_Q_HEADS = 8
N_KV_HEADS = 2
HEAD_DIM = 64
Q_GROUP = N_Q_HEADS // N_KV_HEADS
WINDOW = 128
ROPE_THETA = 500000.0
ROT_DIM = HEAD_DIM // 4
ATTN_WIDTH = N_Q_HEADS * HEAD_DIM
KV_WIDTH = N_KV_HEADS * HEAD_DIM
NEG_INF = -1e30
CONV_WIDTH = D_MODEL // 2
CONV_K = 3
SSM_WIDTH = D_MODEL // 2
SSM_GROUP = 16
SSM_GROUPS = SSM_WIDTH // SSM_GROUP
SSM_STATE = 64
GATE_WIDTH = 3 * D_MODEL
FFN_HIDDEN = 2816
NORM_EPS = 1e-6
LOG2_E = math.log2(math.e)

_Q0 = 0
_KV0 = _Q0 + ATTN_WIDTH
_CB0 = _KV0 + 2 * KV_WIDTH
_CC0 = _CB0 + CONV_WIDTH
_CX0 = _CC0 + CONV_WIDTH
_U0 = _CX0 + CONV_WIDTH
_G0 = _U0 + SSM_WIDTH
IN_COLS = _G0 + GATE_WIDTH

LANES = 128
SUBLANES = 8
BF16_ROWS = 2 * SUBLANES
SSM_CHUNK = 8
SSM_SLABS = SSM_WIDTH // LANES
SLAB_GROUPS = LANES // SSM_GROUP
SLAB_STATE = SLAB_GROUPS * SSM_STATE
TOKEN_TILE = 512
SSM_CHUNK_BLOCK = 256
MATMUL_ROWS = 256
VMEM_LIMIT = 56 * 1024 * 1024

_BF16 = jnp.bfloat16
_F32 = jnp.float32


def _dot(a, b):
    m = a.shape[0]
    if m <= MATMUL_ROWS:
        return jnp.dot(a, b, preferred_element_type=_F32)
    return jnp.concatenate(
        [jnp.dot(a[i:i + MATMUL_ROWS], b, preferred_element_type=_F32) for i in range(0, m, MATMUL_ROWS)], axis=0)


def _sigmoid(x):
    return 0.5 * jnp.tanh(0.5 * x) + 0.5


def _rmsnorm(x, w):
    ms = jnp.mean(x * x, axis=-1, keepdims=True)
    return x * lax.rsqrt(ms + NORM_EPS) * w


def _const_spec(shape):
    return pl.BlockSpec(shape, lambda *_: (0,) * len(shape), pipeline_mode=pl.Buffered(1))


def _layer_spec(shape, layer):
    return pl.BlockSpec((None,) + shape, lambda *_: (layer,) + (0,) * len(shape), pipeline_mode=pl.Buffered(1))


def _cast_plan(weights, layer, grid):
    nsteps = grid[0] * grid[1]
    in_specs, out_specs, out_shapes = [], [], []
    for w in weights:
        _, rows, cols = w.shape
        ncol = next(n for n in (1, 2, 4, 8)
                    if (rows * n) % (nsteps * BF16_ROWS) == 0 and cols % (n * LANES) == 0)
        nrow = nsteps // ncol
        block = (rows // nrow, cols // ncol)

        def index(bi, li, nrow=nrow):
            step = bi * grid[1] + li
            return step % nrow, step // nrow

        in_specs.append(pl.BlockSpec((None,) + block, lambda bi, li, index=index: (layer,) + index(bi, li)))
        out_specs.append(pl.BlockSpec(block, index))
        out_shapes.append(jax.ShapeDtypeStruct((rows, cols), _BF16))
    return in_specs, out_specs, out_shapes


def _with_casts(body, n_in, n_out, n_cast):
    def kernel(*refs):
        ins, rest = refs[:n_in], refs[n_in:]
        cast_src, rest = rest[:n_cast], rest[n_cast:]
        outs, rest = rest[:n_out], rest[n_out:]
        cast_dst, scratch = rest[:n_cast], rest[n_cast:]
        body(*ins, *outs, *scratch)
        for src, dst in zip(cast_src, cast_dst):
            dst[...] = src[...].astype(_BF16)
    return kernel


def _ssm_io_spec(nchunk):
    return pl.BlockSpec((SSM_SLABS, nchunk, None, SSM_CHUNK, LANES), lambda bi, li: (0, li, bi, 0, 0))


def _mixer_in_kernel(sinks_ref, x_ref, nw_ref, win_ref, bg_ref, cos_ref, sa_ref, sb_ref, cw_ref,
                     wao_ref, wco_ref, part_ref, g2_ref, u_ref,
                     kv_scr, z_scr, attn_scr, yc_scr, acc_scr, g0_scr):
    tm = x_ref.shape[1]
    li = pl.program_id(1)

    @pl.when(li == 0)
    def _():
        kv_scr[:, 0:WINDOW, :] = jnp.zeros((8, WINDOW, LANES), _BF16)
        z_scr[0:SUBLANES, :] = jnp.zeros((SUBLANES, CONV_WIDTH), _F32)

    x = x_ref[0]
    h = _rmsnorm(x, nw_ref[...]).astype(_BF16)

    cos = cos_ref[...]
    sa = sa_ref[...]
    sb = sb_ref[...]

    def rope(t):
        return (t * cos + pltpu.roll(t, LANES - ROT_DIM // 2, axis=1) * sa
                + pltpu.roll(t, ROT_DIM // 2, axis=1) * sb)

    kv = _dot(h, win_ref[:, _KV0:_KV0 + 2 * KV_WIDTH])
    lane = lax.broadcasted_iota(jnp.int32, (tm, LANES), 1)
    lo = lane < HEAD_DIM
    for kind in range(2):
        t = kv[:, kind * LANES:(kind + 1) * LANES]
        if kind == 0:
            t = rope(t)
        t_sw = pltpu.roll(t, HEAD_DIM, axis=1)
        zero = jnp.zeros_like(t)
        variants = (jnp.where(lo, t, zero), jnp.where(lo, zero, t_sw),
                    jnp.where(lo, t_sw, zero), jnp.where(lo, zero, t))
        for n, val in enumerate(variants):
            kv_scr[kind * 4 + n, WINDOW:WINDOW + tm, :] = val.astype(_BF16)

    q = _dot(h, win_ref[:, _Q0:_Q0 + ATTN_WIDTH])
    row = lax.broadcasted_iota(jnp.int32, (WINDOW, 2 * WINDOW), 0)
    col = lax.broadcasted_iota(jnp.int32, (WINDOW, 2 * WINDOW), 1)
    band_ok = (col > row) & (col <= row + WINDOW)
    first_ok = band_ok & (col >= jnp.where(li > 0, 0, WINDOW))

    def gate(n):
        c0 = _G0 + n * D_MODEL
        return _sigmoid(_dot(h, win_ref[:, c0:c0 + D_MODEL]) + bg_ref[:, n * D_MODEL:(n + 1) * D_MODEL])


    cb = _dot(h, win_ref[:, _CB0:_CB0 + CONV_WIDTH])
    cc = _dot(h, win_ref[:, _CC0:_CC0 + CONV_WIDTH])
    cx = _dot(h, win_ref[:, _CX0:_CX0 + CONV_WIDTH])
    z = cc * cx
    z_scr[SUBLANES:SUBLANES + tm, :] = z
    z1 = z_scr[SUBLANES - 1:SUBLANES - 1 + tm, :]
    z2 = z_scr[SUBLANES - 2:SUBLANES - 2 + tm, :]
    conv = cw_ref[0:1, :] * z2 + cw_ref[1:2, :] * z1 + cw_ref[2:3, :] * z
    yc_scr[...] = (cb * conv).astype(_BF16)
    acc_scr[...] = gate(1) * _dot(yc_scr[...], wco_ref[...])

    g0_scr[...] = gate(0)
    u = _dot(h, win_ref[:, _U0:_U0 + SSM_WIDTH])
    nchunk = tm // SSM_CHUNK
    for sl in range(SSM_SLABS):
        u_ref[sl] = u[:, sl * LANES:(sl + 1) * LANES].reshape(nchunk, SSM_CHUNK, LANES)

    for m in range(N_Q_HEADS // 2):
        qp = (rope(q[:, m * LANES:(m + 1) * LANES]) * (HEAD_DIM ** -0.5 * LOG2_E)).astype(_BF16)
        j = (2 * m) // Q_GROUP
        for r in range(tm // WINDOW):
            qb = qp[r * WINDOW:(r + 1) * WINDOW]
            mask = first_ok if r == 0 else band_ok
            o = None
            for half in range(2):
                sink = sinks_ref[2 * m + half] * LOG2_E
                kb = kv_scr[j * 2 + half, r * WINDOW:(r + 2) * WINDOW, :]
                vb = kv_scr[4 + j * 2 + half, r * WINDOW:(r + 2) * WINDOW, :]
                s = lax.dot_general(qb, kb, (((1,), (1,)), ((), ())), preferred_element_type=_F32)
                s = jnp.where(mask, s, NEG_INF)
                mx = jnp.maximum(jnp.max(s, axis=-1, keepdims=True), sink)
                p = jnp.exp2(s - mx)
                denom = jnp.sum(p, axis=-1, keepdims=True) + jnp.exp2(sink - mx)
                oh = _dot(p.astype(_BF16), vb) * (1.0 / denom)
                o = oh if o is None else o + oh
            attn_scr[r * WINDOW:(r + 1) * WINDOW, m * LANES:(m + 1) * LANES] = o.astype(_BF16)

    g2_ref[0] = gate(2).astype(_BF16)

    y_attn = _dot(attn_scr[...], wao_ref[...])
    part_ref[0] = (g0_scr[...] * y_attn + acc_scr[...]).astype(_BF16)

    kv_scr[:, 0:WINDOW, :] = kv_scr[:, tm:tm + WINDOW, :]
    z_scr[0:SUBLANES, :] = z_scr[tm:tm + SUBLANES, :]


def _mixer_in(x, sinks, nw, win, bg, cos_t, sa_t, sb_t, cw, wao, wco, layer, cast_weights):
    b, l, _ = x.shape
    tm = TOKEN_TILE
    nchunk = tm // SSM_CHUNK
    grid = (b, l // tm)
    in_specs = [
        pl.BlockSpec(memory_space=pltpu.SMEM),
        pl.BlockSpec((1, tm, D_MODEL), lambda bi, li: (bi, li, 0)),
        _layer_spec((1, D_MODEL), layer),
        _const_spec((D_MODEL, IN_COLS)),
        _layer_spec((1, GATE_WIDTH), layer),
        pl.BlockSpec((tm, LANES), lambda bi, li: (li, 0)),
        pl.BlockSpec((tm, LANES), lambda bi, li: (li, 0)),
        pl.BlockSpec((tm, LANES), lambda bi, li: (li, 0)),
        _layer_spec((CONV_K, CONV_WIDTH), layer),
        _const_spec((ATTN_WIDTH, D_MODEL)),
        _const_spec((CONV_WIDTH, D_MODEL)),
    ]
    out_specs = [
        pl.BlockSpec((1, tm, D_MODEL), lambda bi, li: (bi, li, 0)),
        pl.BlockSpec((1, tm, D_MODEL), lambda bi, li: (bi, li, 0)),
        _ssm_io_spec(nchunk),
    ]
    out_shape = [
        jax.ShapeDtypeStruct((b, l, D_MODEL), _BF16),
        jax.ShapeDtypeStruct((b, l, D_MODEL), _BF16),
        jax.ShapeDtypeStruct((SSM_SLABS, l // SSM_CHUNK, b, SSM_CHUNK, LANES), _F32),
    ]
    scratch = [
        pltpu.VMEM((8, WINDOW + tm, LANES), _BF16),
        pltpu.VMEM((SUBLANES + tm, CONV_WIDTH), _F32),
        pltpu.VMEM((tm, ATTN_WIDTH), _BF16),
        pltpu.VMEM((tm, CONV_WIDTH), _BF16),
        pltpu.VMEM((tm, D_MODEL), _F32),
        pltpu.VMEM((tm, D_MODEL), _F32),
    ]
    cast_in, cast_out, cast_shape = _cast_plan(cast_weights, layer, grid)
    return pl.pallas_call(
        _with_casts(_mixer_in_kernel, len(in_specs), len(out_specs), len(cast_weights)),
        grid=grid, in_specs=in_specs + cast_in, out_specs=out_specs + cast_out,
        out_shape=out_shape + cast_shape, scratch_shapes=scratch, name="mixer_in",
        compiler_params=pltpu.CompilerParams(
            dimension_semantics=("arbitrary", "arbitrary"), vmem_limit_bytes=VMEM_LIMIT),
    )(sinks, x, nw, win, bg, cos_t, sa_t, sb_t, cw, wao, wco, *cast_weights)


def _ssm_kernel(u_ref, wc_ref, cct_ref, kt_ref, d_ref, ar_ref, ai_ref, y_ref,
                mb_scr, mt_scr, mct_scr, ucat_scr, s_scr, xr_scr, xi_scr, *, batch):
    t = SSM_CHUNK
    rows = u_ref.shape[1] // t

    @pl.when(pl.program_id(1) == 0)
    def _():
        chan_bits, state_bits = SSM_GROUP.bit_length() - 1, SSM_STATE.bit_length() - 1
        r_grp = lax.broadcasted_iota(jnp.int32, (LANES, 2 * SLAB_STATE), 0) >> chan_bits
        c_grp = (lax.broadcasted_iota(jnp.int32, (LANES, 2 * SLAB_STATE), 1) & (SLAB_STATE - 1)) >> state_bits
        same_state = r_grp == c_grp

        def expand(tbl):
            return jnp.where(same_state, jnp.tile(tbl, (SLAB_GROUPS, 1)), jnp.zeros((), tbl.dtype))

        for st in range(t):
            rs = slice(st * LANES, (st + 1) * LANES)
            mb_scr[rs, :] = expand(wc_ref[0, st]).astype(_BF16)
            mct_scr[rs, :] = expand(cct_ref[0, st]).astype(_BF16)

        lane_grp = lax.broadcasted_iota(jnp.int32, (SSM_GROUP, LANES), 1) >> chan_bits
        on_diag = (lax.broadcasted_iota(jnp.int32, (LANES, LANES), 0)
                   == lax.broadcasted_iota(jnp.int32, (LANES, LANES), 1))
        zero_blk = jnp.zeros((LANES, LANES), _BF16)
        blks = []
        for lag in range(t):
            pieces = []
            for gl in range(SLAB_GROUPS):
                shift = ((gl - lag) * SSM_GROUP) % LANES
                piece = kt_ref[0, gl]
                if shift:
                    piece = pltpu.roll(piece, shift, axis=1)
                pieces.append(jnp.where(lane_grp == gl, piece, 0.0))
            blk = jnp.concatenate(pieces, axis=0)
            if lag == 0:
                blk = blk + jnp.where(on_diag, d_ref[0], 0.0)
            blks.append(blk.astype(_BF16))
        for s_in in range(t):
            for s_out in range(t):
                mt_scr[s_in * LANES:(s_in + 1) * LANES, s_out * LANES:(s_out + 1) * LANES] = (
                    blks[s_out - s_in] if s_out >= s_in else zero_blk)
        xr_scr[...] = jnp.zeros_like(xr_scr)
        xi_scr[...] = jnp.zeros_like(xi_scr)

    for st in range(t):
        ucat_scr[:, st * LANES:(st + 1) * LANES] = u_ref[0, pl.ds(st, rows, stride=t), :].astype(_BF16)

    s_scr[...] = _dot(ucat_scr[...], mb_scr[...])

    ar = jnp.broadcast_to(ar_ref[0], (batch, SLAB_STATE))
    ai = jnp.broadcast_to(ai_ref[0], (batch, SLAB_STATE))

    xr, xi = xr_scr[...], xi_scr[...]
    for c in range(rows // batch):
        rs = slice(c * batch, (c + 1) * batch)
        sr = s_scr[rs, 0:SLAB_STATE]
        si = s_scr[rs, SLAB_STATE:2 * SLAB_STATE]
        s_scr[rs, 0:SLAB_STATE] = xr
        s_scr[rs, SLAB_STATE:2 * SLAB_STATE] = xi
        xr, xi = ar * xr - ai * xi + sr, ar * xi + ai * xr + si
    xr_scr[...] = xr
    xi_scr[...] = xi

    pair = 2 * LANES
    toeplitz = jnp.concatenate(
        [_dot(ucat_scr[:, 0:c0 + pair], mt_scr[0:c0 + pair, c0:c0 + pair]) for c0 in range(0, t * LANES, pair)],
        axis=1)
    y = toeplitz + lax.dot_general(
        s_scr[...].astype(_BF16), mct_scr[...], (((1,), (1,)), ((), ())), preferred_element_type=_F32)
    for st in range(t):
        y_ref[0, pl.ds(st, rows, stride=t), :] = y[:, st * LANES:(st + 1) * LANES]


def _ssm(u, wc, cct, kt, dd, ar, ai, layer, batch):
    nslab, tok_rows, _ = u.shape
    blk = SSM_CHUNK_BLOCK * batch * SSM_CHUNK
    rows = SSM_CHUNK_BLOCK * batch
    width = SSM_CHUNK * LANES
    io = pl.BlockSpec((1, blk, LANES), lambda q, ci: (q, ci, 0))
    par = lambda *shape: pl.BlockSpec((None, 1) + shape, lambda q, ci: (layer, q) + (0,) * len(shape))
    return pl.pallas_call(
        functools.partial(_ssm_kernel, batch=batch),
        grid=(nslab, tok_rows // blk),
        in_specs=[io, par(SSM_CHUNK, SSM_GROUP, 2 * SLAB_STATE), par(SSM_CHUNK, SSM_GROUP, 2 * SLAB_STATE),
                  par(SLAB_GROUPS, SSM_GROUP, SSM_CHUNK * SSM_GROUP), par(1, LANES),
                  par(1, SLAB_STATE), par(1, SLAB_STATE)],
        out_specs=io,
        out_shape=jax.ShapeDtypeStruct(u.shape, _F32),
        scratch_shapes=[
            pltpu.VMEM((width, 2 * SLAB_STATE), _BF16),
            pltpu.VMEM((width, width), _BF16),
            pltpu.VMEM((width, 2 * SLAB_STATE), _BF16),
            pltpu.VMEM((rows, width), _BF16),
            pltpu.VMEM((rows, 2 * SLAB_STATE), _F32),
            pltpu.VMEM((batch, SLAB_STATE), _F32),
            pltpu.VMEM((batch, SLAB_STATE), _F32),
        ],
        name="ssm",
        compiler_params=pltpu.CompilerParams(
            dimension_semantics=("arbitrary", "arbitrary"), vmem_limit_bytes=VMEM_LIMIT),
    )(u, wc, cct, kt, dd, ar, ai)


def _ssm_operators(a_re, a_im, b_re, b_im, c_re, c_im, d, log_dt):
    t = SSM_CHUNK
    p, hh, nq, ng = SSM_STATE, SSM_GROUP, SSM_SLABS, SLAB_GROUPS
    depth = a_re.shape[0]
    hi = lax.Precision.HIGHEST
    a_re, a_im = a_re.astype(_F32), a_im.astype(_F32)
    dt = jnp.exp(log_dt.astype(_F32))[:, :, None]
    b_re, b_im = b_re.astype(_F32), b_im.astype(_F32)
    c_re, c_im = c_re.astype(_F32), c_im.astype(_F32)

    def powers(ar_, ai_, dt_, n):
        mag, ang = jnp.exp(ar_ * dt_ * n), ai_ * dt_ * n
        return mag * jnp.cos(ang), mag * jnp.sin(ang)

    def zoh(ar_, ai_, dt_):
        lr, li_ = powers(ar_, ai_, dt_, 1.0)
        nr, den = lr - 1.0, ar_ * ar_ + ai_ * ai_
        return (nr * ar_ + li_ * ai_) / den, (li_ * ar_ - nr * ai_) / den

    steps = jnp.arange(t, dtype=_F32)[:, None, None]
    pr, pi = powers(a_re[:, :, None, None], a_im[:, :, None, None], dt[:, :, None, None], steps)
    cr, ci = c_re[:, :, None], c_im[:, :, None]
    lhs = jnp.concatenate([cr * pr - ci * pi, -(cr * pi + ci * pr)], axis=-1)
    f_re, f_im = zoh(a_re, a_im, dt)
    rhs = jnp.concatenate([f_re[..., None] * b_re - f_im[..., None] * b_im,
                           f_re[..., None] * b_im + f_im[..., None] * b_re], axis=-2)
    kt = jnp.einsum("ygck,ygmc->ygkm", rhs, lhs.reshape(depth, -1, t * hh, 2 * p), precision=hi)
    kt = kt.reshape(depth, nq, ng, hh, t * hh)
    dd = d.astype(_F32).reshape(depth, nq, 1, LANES)

    lane = lambda v: v.reshape(depth, nq, 1, ng * p)
    la_re, la_im = lane(a_re), lane(a_im)
    ldt = lane(jnp.broadcast_to(dt, a_re.shape))
    lf_re, lf_im = zoh(la_re, la_im, ldt)
    to_rows = lambda v, axes: v.reshape(depth, nq, ng, *v.shape[2:]).transpose(axes).reshape(depth, nq, hh, ng * p)
    lb_re, lb_im = to_rows(b_re, (0, 1, 4, 2, 3)), to_rows(b_im, (0, 1, 4, 2, 3))
    lc_re, lc_im = to_rows(c_re, (0, 1, 3, 2, 4)), to_rows(c_im, (0, 1, 3, 2, 4))
    bb_re, bb_im = lf_re * lb_re - lf_im * lb_im, lf_re * lb_im + lf_im * lb_re

    rev = (t - 1 - jnp.arange(t, dtype=_F32))[:, None, None]
    rr, ri = powers(la_re[:, :, None], la_im[:, :, None], ldt[:, :, None], rev)
    wc = jnp.concatenate([rr * bb_re[:, :, None] - ri * bb_im[:, :, None],
                          rr * bb_im[:, :, None] + ri * bb_re[:, :, None]], axis=-1)
    fwd = (1.0 + jnp.arange(t, dtype=_F32))[:, None, None]
    fr, fi = powers(la_re[:, :, None], la_im[:, :, None], ldt[:, :, None], fwd)
    cct = jnp.concatenate([lc_re[:, :, None] * fr - lc_im[:, :, None] * fi,
                           -(lc_re[:, :, None] * fi + lc_im[:, :, None] * fr)], axis=-1)
    ar, ai = powers(la_re, la_im, ldt, float(t))
    return wc.astype(_BF16), cct.astype(_BF16), kt, dd, ar, ai


def _mixer_out_kernel(x_ref, part_ref, g2_ref, y_ref, wglu_ref, wso_ref, wmix_ref, nf_ref,
                      wfi_ref, wfo_ref, nlast_ref, out_ref, *, final):
    tm = x_ref.shape[1]
    y = jnp.concatenate([y_ref[sl].reshape(tm, LANES) for sl in range(SSM_SLABS)], axis=1)
    ys = 0.5 * y * (1.0 + jnp.tanh(math.sqrt(2.0 / math.pi) * (y + 0.044715 * (y * y * y))))
    glu = ys * _sigmoid(_dot(ys.astype(_BF16), wglu_ref[...]))
    y_ssm = _dot(glu.astype(_BF16), wso_ref[...])
    merged = part_ref[0].astype(_F32) + g2_ref[0].astype(_F32) * y_ssm
    x1 = x_ref[0] + _dot(merged.astype(_BF16), wmix_ref[...])

    h = _rmsnorm(x1, nf_ref[...]).astype(_BF16)
    gt = _dot(h, wfi_ref[:, 0:FFN_HIDDEN])
    up = _dot(h, wfi_ref[:, FFN_HIDDEN:2 * FFN_HIDDEN])
    act = (gt * _sigmoid(gt) * up).astype(_BF16)
    x2 = x1 + _dot(act, wfo_ref[...])
    if final:
        x2 = _rmsnorm(x2, nlast_ref[...])
    out_ref[0] = x2


def _mixer_out(x, part, g2, y, wglu, wso, wmix, nf, wfi, wfo, nlast, layer, final, cast_weights):
    b, l, _ = x.shape
    tm = TOKEN_TILE
    nchunk = tm // SSM_CHUNK
    grid = (b, l // tm)
    tok = pl.BlockSpec((1, tm, D_MODEL), lambda bi, li: (bi, li, 0))
    in_specs = [
        tok, tok, tok,
        _ssm_io_spec(nchunk),
        _const_spec((SSM_WIDTH, SSM_WIDTH)),
        _const_spec((SSM_WIDTH, D_MODEL)),
        _const_spec((D_MODEL, D_MODEL)),
        _layer_spec((1, D_MODEL), layer),
        _const_spec((D_MODEL, 2 * FFN_HIDDEN)),
        _const_spec((FFN_HIDDEN, D_MODEL)),
        _const_spec((1, D_MODEL)),
    ]
    cast_in, cast_out, cast_shape = _cast_plan(cast_weights, layer + 1, grid)
    body = functools.partial(_mixer_out_kernel, final=final)
    return pl.pallas_call(
        _with_casts(body, len(in_specs), 1, len(cast_weights)),
        grid=grid, in_specs=in_specs + cast_in, out_specs=[tok] + cast_out,
        out_shape=[jax.ShapeDtypeStruct((b, l, D_MODEL), _F32)] + cast_shape,
        name="mixer_out",
        compiler_params=pltpu.CompilerParams(
            dimension_semantics=("arbitrary", "arbitrary"), vmem_limit_bytes=VMEM_LIMIT),
    )(x, part, g2, y, wglu, wso, wmix, nf, wfi, wfo, nlast, *cast_weights)


def _rope_tables(seq_len):
    half = ROT_DIM // 2
    pos = jnp.arange(seq_len, dtype=_F32)
    inv_freq = ROPE_THETA ** (-jnp.arange(0, ROT_DIM, 2, dtype=_F32) / ROT_DIM)
    ang = pos[:, None] * inv_freq[None, :]
    cos, sin = jnp.cos(ang), jnp.sin(ang)
    ones = jnp.ones((seq_len, HEAD_DIM - ROT_DIM), _F32)
    zeros_h = jnp.zeros((seq_len, half), _F32)
    zeros_r = jnp.zeros((seq_len, HEAD_DIM - ROT_DIM), _F32)
    cos_h = jnp.concatenate([cos, cos, ones], axis=1)
    sa_h = jnp.concatenate([-sin, zeros_h, zeros_r], axis=1)
    sb_h = jnp.concatenate([zeros_h, sin, zeros_r], axis=1)
    rep = LANES // HEAD_DIM
    return jnp.tile(cos_h, (1, rep)), jnp.tile(sa_h, (1, rep)), jnp.tile(sb_h, (1, rep))


def kernel(x, norm_mix, w_in, b_gate, attn_sinks, w_attn_o, conv_w, w_conv_o, ssm_a_re, ssm_a_im,
           ssm_b_re, ssm_b_im, ssm_c_re, ssm_c_im, ssm_d, ssm_log_dt, w_ssm_glu, w_ssm_o, w_mix_o,
           norm_ffn, w_ffn_in, w_ffn_out, norm_final):
    b, l, d_model = x.shape
    depth = w_in.shape[0]
    assert d_model == D_MODEL and l % TOKEN_TILE == 0 and b % SUBLANES == 0
    assert l % (SSM_CHUNK * SSM_CHUNK_BLOCK) == 0
    assert w_in.shape[2] == IN_COLS and w_ffn_out.shape[1] == FFN_HIDDEN
    cos_t, sa_t, sb_t = _rope_tables(l)
    f32 = lambda p: p.astype(_F32)
    norm_mix, norm_ffn = f32(norm_mix)[:, None, :], f32(norm_ffn)[:, None, :]
    b_gate, conv_w, attn_sinks = f32(b_gate)[:, None, :], f32(conv_w), f32(attn_sinks)
    nlast = f32(norm_final).reshape(1, D_MODEL)
    wc, cct, kt, dd, ar, ai = _ssm_operators(
        ssm_a_re, ssm_a_im, ssm_b_re, ssm_b_im, ssm_c_re, ssm_c_im, ssm_d, ssm_log_dt)
    in_weights = tuple(f32(w) for w in (w_in, w_attn_o, w_conv_o))
    out_weights = tuple(f32(w) for w in (w_ssm_glu, w_ssm_o, w_mix_o, w_ffn_in, w_ffn_out))
    w_mixer_in = [w[0].astype(_BF16) for w in in_weights]
    x = f32(x)
    for i in range(depth):
        last = i == depth - 1
        win, wao, wco = w_mixer_in
        part, g2, u, *w_mixer_out = _mixer_in(
            x, attn_sinks[i], norm_mix, win, b_gate, cos_t, sa_t, sb_t, conv_w, wao, wco, i, out_weights)
        y = _ssm(u.reshape(SSM_SLABS, l * b, LANES), wc, cct, kt, dd, ar, ai, i, b).reshape(u.shape)
        x, *w_mixer_in = _mixer_out(
            x, part, g2, y, *w_mixer_out[:3], norm_ffn, *w_mixer_out[3:], nlast, i, last,
            () if last else in_weights)
    return x
```

```python
import functools
import math

import jax
import jax.numpy as jnp
from jax import lax
from jax.experimental import pallas as pl
from jax.experimental.pallas import tpu as pltpu

D_MODEL = 1024
N_Q_HEADS = 8
N_KV_HEADS = 2
HEAD_DIM = 64
Q_GROUP = N_Q_HEADS // N_KV_HEADS
WINDOW = 128
ROPE_THETA = 500000.0
ROT_DIM = HEAD_DIM // 4
ATTN_WIDTH = N_Q_HEADS * HEAD_DIM
KV_WIDTH = N_KV_HEADS * HEAD_DIM
NEG_INF = -1e30
CONV_WIDTH = D_MODEL // 2
CONV_K = 3
SSM_WIDTH = D_MODEL // 2
SSM_GROUP = 16
SSM_GROUPS = SSM_WIDTH // SSM_GROUP
SSM_STATE = 64
GATE_WIDTH = 3 * D_MODEL
FFN_HIDDEN = 2816
NORM_EPS = 1e-6
LOG2_E = math.log2(math.e)

_Q0 = 0
_KV0 = _Q0 + ATTN_WIDTH
_CB0 = _KV0 + 2 * KV_WIDTH
_CC0 = _CB0 + CONV_WIDTH
_CX0 = _CC0 + CONV_WIDTH
_U0 = _CX0 + CONV_WIDTH
_G0 = _U0 + SSM_WIDTH
IN_COLS = _G0 + GATE_WIDTH

LANES = 128
SUBLANES = 8
BF16_ROWS = 2 * SUBLANES
SSM_CHUNK = 8
SSM_SLABS = SSM_WIDTH // LANES
SLAB_GROUPS = LANES // SSM_GROUP
SLAB_STATE = SLAB_GROUPS * SSM_STATE
TOKEN_TILE = 512
SSM_CHUNK_BLOCK = 128
MATMUL_ROWS = 256
VMEM_LIMIT = 56 * 1024 * 1024

_BF16 = jnp.bfloat16
_F32 = jnp.float32


def _dot(a, b):
    m = a.shape[0]
    if m <= MATMUL_ROWS:
        return jnp.dot(a, b, preferred_element_type=_F32)
    return jnp.concatenate(
        [jnp.dot(a[i:i + MATMUL_ROWS], b, preferred_element_type=_F32) for i in range(0, m, MATMUL_ROWS)], axis=0)


def _sigmoid(x):
    return 0.5 * jnp.tanh(0.5 * x) + 0.5


def _rmsnorm(x, w):
    ms = jnp.mean(x * x, axis=-1, keepdims=True)
    return x * lax.rsqrt(ms + NORM_EPS) * w


def _const_spec(shape):
    return pl.BlockSpec(shape, lambda *_: (0,) * len(shape), pipeline_mode=pl.Buffered(1))


def _layer_spec(shape, layer):
    return pl.BlockSpec((None,) + shape, lambda *_: (layer,) + (0,) * len(shape), pipeline_mode=pl.Buffered(1))


def _cast_plan(weights, layer, grid):
    nsteps = grid[0] * grid[1]
    in_specs, out_specs, out_shapes = [], [], []
    for w in weights:
        _, rows, cols = w.shape
        ncol = next(n for n in (1, 2, 4, 8)
                    if (rows * n) % (nsteps * BF16_ROWS) == 0 and cols % (n * LANES) == 0)
        nrow = nsteps // ncol
        block = (rows // nrow, cols // ncol)

        def index(bi, li, nrow=nrow):
            step = bi * grid[1] + li
            return step % nrow, step // nrow

        in_specs.append(pl.BlockSpec((None,) + block, lambda bi, li, index=index: (layer,) + index(bi, li)))
        out_specs.append(pl.BlockSpec(block, index))
        out_shapes.append(jax.ShapeDtypeStruct((rows, cols), _BF16))
    return in_specs, out_specs, out_shapes


def _with_casts(body, n_in, n_out, n_cast):
    def kernel(*refs):
        ins, rest = refs[:n_in], refs[n_in:]
        cast_src, rest = rest[:n_cast], rest[n_cast:]
        outs, rest = rest[:n_out], rest[n_out:]
        cast_dst, scratch = rest[:n_cast], rest[n_cast:]
        body(*ins, *outs, *scratch)
        for src, dst in zip(cast_src, cast_dst):
            dst[...] = src[...].astype(_BF16)
    return kernel


def _ssm_io_spec(nchunk):
    return pl.BlockSpec((SSM_SLABS, nchunk, None, SSM_CHUNK, LANES), lambda bi, li: (0, li, bi, 0, 0))


def _mixer_in_kernel(sinks_ref, x_ref, nw_ref, win_ref, bg_ref, cos_ref, sa_ref, sb_ref, cw_ref,
                     wco_ref, attn_ref, g0_ref, acc_ref, g2_ref, u_ref,
                     kv_scr, z_scr, yc_scr):
    tm = x_ref.shape[1]
    li = pl.program_id(1)

    @pl.when(li == 0)
    def _():
        kv_scr[:, 0:WINDOW, :] = jnp.zeros((8, WINDOW, LANES), _BF16)
        z_scr[0:SUBLANES, :] = jnp.zeros((SUBLANES, CONV_WIDTH), _F32)

    x = x_ref[0]
    h = _rmsnorm(x, nw_ref[...]).astype(_BF16)

    cos = cos_ref[...]
    sa = sa_ref[...]
    sb = sb_ref[...]

    def rope(t):
        return (t * cos + pltpu.roll(t, LANES - ROT_DIM // 2, axis=1) * sa
                + pltpu.roll(t, ROT_DIM // 2, axis=1) * sb)

    kv = _dot(h, win_ref[:, _KV0:_KV0 + 2 * KV_WIDTH])
    lane = lax.broadcasted_iota(jnp.int32, (tm, LANES), 1)
    lo = lane < HEAD_DIM
    for kind in range(2):
        t = kv[:, kind * LANES:(kind + 1) * LANES]
        if kind == 0:
            t = rope(t)
        t_sw = pltpu.roll(t, HEAD_DIM, axis=1)
        zero = jnp.zeros_like(t)
        variants = (jnp.where(lo, t, zero), jnp.where(lo, zero, t_sw),
                    jnp.where(lo, t_sw, zero), jnp.where(lo, zero, t))
        for n, val in enumerate(variants):
            kv_scr[kind * 4 + n, WINDOW:WINDOW + tm, :] = val.astype(_BF16)

    q = _dot(h, win_ref[:, _Q0:_Q0 + ATTN_WIDTH])
    row = lax.broadcasted_iota(jnp.int32, (WINDOW, 2 * WINDOW), 0)
    col = lax.broadcasted_iota(jnp.int32, (WINDOW, 2 * WINDOW), 1)
    band_ok = (col > row) & (col <= row + WINDOW)
    first_ok = band_ok & (col >= jnp.where(li > 0, 0, WINDOW))

    def gate(n):
        c0 = _G0 + n * D_MODEL
        return _sigmoid(_dot(h, win_ref[:, c0:c0 + D_MODEL]) + bg_ref[:, n * D_MODEL:(n + 1) * D_MODEL])


    cb = _dot(h, win_ref[:, _CB0:_CB0 + CONV_WIDTH])
    cc = _dot(h, win_ref[:, _CC0:_CC0 + CONV_WIDTH])
    cx = _dot(h, win_ref[:, _CX0:_CX0 + CONV_WIDTH])
    z = cc * cx
    z_scr[SUBLANES:SUBLANES + tm, :] = z
    z1 = z_scr[SUBLANES - 1:SUBLANES - 1 + tm, :]
    z2 = z_scr[SUBLANES - 2:SUBLANES - 2 + tm, :]
    conv = cw_ref[0:1, :] * z2 + cw_ref[1:2, :] * z1 + cw_ref[2:3, :] * z
    yc_scr[...] = (cb * conv).astype(_BF16)
    acc_ref[0] = (gate(1) * _dot(yc_scr[...], wco_ref[...])).astype(_BF16)

    g0_ref[0] = gate(0).astype(_BF16)
    u = _dot(h, win_ref[:, _U0:_U0 + SSM_WIDTH])
    nchunk = tm // SSM_CHUNK
    for sl in range(SSM_SLABS):
        u_ref[sl] = u[:, sl * LANES:(sl + 1) * LANES].reshape(nchunk, SSM_CHUNK, LANES)

    for m in range(N_Q_HEADS // 2):
        qp = (rope(q[:, m * LANES:(m + 1) * LANES]) * (HEAD_DIM ** -0.5 * LOG2_E)).astype(_BF16)
        j = (2 * m) // Q_GROUP
        for r in range(tm // WINDOW):
            qb = qp[r * WINDOW:(r + 1) * WINDOW]
            mask = first_ok if r == 0 else band_ok
            o = None
            for half in range(2):
                sink = sinks_ref[2 * m + half] * LOG2_E
                kb = kv_scr[j * 2 + half, r * WINDOW:(r + 2) * WINDOW, :]
                vb = kv_scr[4 + j * 2 + half, r * WINDOW:(r + 2) * WINDOW, :]
                s = lax.dot_general(qb, kb, (((1,), (1,)), ((), ())), preferred_element_type=_F32)
                s = jnp.where(mask, s, NEG_INF)
                mx = jnp.maximum(jnp.max(s, axis=-1, keepdims=True), sink)
                p = jnp.exp2(s - mx)
                denom = jnp.sum(p, axis=-1, keepdims=True) + jnp.exp2(sink - mx)
                oh = _dot(p.astype(_BF16), vb) * (1.0 / denom)
                o = oh if o is None else o + oh
            attn_ref[0, r * WINDOW:(r + 1) * WINDOW, m * LANES:(m + 1) * LANES] = o.astype(_BF16)

    g2_ref[0] = gate(2).astype(_BF16)

    kv_scr[:, 0:WINDOW, :] = kv_scr[:, tm:tm + WINDOW, :]
    z_scr[0:SUBLANES, :] = z_scr[tm:tm + SUBLANES, :]


def _mixer_in(x, sinks, nw, win, bg, cos_t, sa_t, sb_t, cw, wco, layer, cast_weights):
    b, l, _ = x.shape
    tm = TOKEN_TILE
    nchunk = tm // SSM_CHUNK
    grid = (b, l // tm)
    in_specs = [
        pl.BlockSpec(memory_space=pltpu.SMEM),
        pl.BlockSpec((1, tm, D_MODEL), lambda bi, li: (bi, li, 0)),
        _layer_spec((1, D_MODEL), layer),
        _const_spec((D_MODEL, IN_COLS)),
        _layer_spec((1, GATE_WIDTH), layer),
        pl.BlockSpec((tm, LANES), lambda bi, li: (li, 0)),
        pl.BlockSpec((tm, LANES), lambda bi, li: (li, 0)),
        pl.BlockSpec((tm, LANES), lambda bi, li: (li, 0)),
        _layer_spec((CONV_K, CONV_WIDTH), layer),
        _const_spec((CONV_WIDTH, D_MODEL)),
    ]
    tok = pl.BlockSpec((1, tm, D_MODEL), lambda bi, li: (bi, li, 0))
    out_specs = [
        pl.BlockSpec((1, tm, ATTN_WIDTH), lambda bi, li: (bi, li, 0)),
        tok, tok, tok,
        _ssm_io_spec(nchunk),
    ]
    out_shape = [
        jax.ShapeDtypeStruct((b, l, ATTN_WIDTH), _BF16),
        jax.ShapeDtypeStruct((b, l, D_MODEL), _BF16),
        jax.ShapeDtypeStruct((b, l, D_MODEL), _BF16),
        jax.ShapeDtypeStruct((b, l, D_MODEL), _BF16),
        jax.ShapeDtypeStruct((SSM_SLABS, l // SSM_CHUNK, b, SSM_CHUNK, LANES), _F32),
    ]
    scratch = [
        pltpu.VMEM((8, WINDOW + tm, LANES), _BF16),
        pltpu.VMEM((SUBLANES + tm, CONV_WIDTH), _F32),
        pltpu.VMEM((tm, CONV_WIDTH), _BF16),
    ]
    cast_in, cast_out, cast_shape = _cast_plan(cast_weights, layer, grid)
    return pl.pallas_call(
        _with_casts(_mixer_in_kernel, len(in_specs), len(out_specs), len(cast_weights)),
        grid=grid, in_specs=in_specs + cast_in, out_specs=out_specs + cast_out,
        out_shape=out_shape + cast_shape, scratch_shapes=scratch, name="mixer_in",
        compiler_params=pltpu.CompilerParams(
            dimension_semantics=("arbitrary", "arbitrary"), vmem_limit_bytes=VMEM_LIMIT),
    )(sinks, x, nw, win, bg, cos_t, sa_t, sb_t, cw, wco, *cast_weights)


def _ssm_kernel(u_ref, wc_ref, cct_ref, kt_ref, d_ref, ar_ref, ai_ref, y_ref,
                mb_scr, mt_scr, mct_scr, ucat_scr, s_scr, xr_scr, xi_scr, *, batch):
    t = SSM_CHUNK
    rows = u_ref.shape[1] // t

    @pl.when(pl.program_id(1) == 0)
    def _():
        chan_bits, state_bits = SSM_GROUP.bit_length() - 1, SSM_STATE.bit_length() - 1
        r_grp = lax.broadcasted_iota(jnp.int32, (LANES, 2 * SLAB_STATE), 0) >> chan_bits
        c_grp = (lax.broadcasted_iota(jnp.int32, (LANES, 2 * SLAB_STATE), 1) & (SLAB_STATE - 1)) >> state_bits
        same_state = r_grp == c_grp

        def expand(tbl):
            return jnp.where(same_state, jnp.tile(tbl, (SLAB_GROUPS, 1)), jnp.zeros((), tbl.dtype))

        for st in range(t):
            rs = slice(st * LANES, (st + 1) * LANES)
            mb_scr[rs, :] = expand(wc_ref[0, st]).astype(_BF16)
            mct_scr[rs, :] = expand(cct_ref[0, st]).astype(_BF16)

        lane_grp = lax.broadcasted_iota(jnp.int32, (SSM_GROUP, LANES), 1) >> chan_bits
        on_diag = (lax.broadcasted_iota(jnp.int32, (LANES, LANES), 0)
                   == lax.broadcasted_iota(jnp.int32, (LANES, LANES), 1))
        zero_blk = jnp.zeros((LANES, LANES), _BF16)
        blks = []
        for lag in range(t):
            pieces = []
            for gl in range(SLAB_GROUPS):
                shift = ((gl - lag) * SSM_GROUP) % LANES
                piece = kt_ref[0, gl]
                if shift:
                    piece = pltpu.roll(piece, shift, axis=1)
                pieces.append(jnp.where(lane_grp == gl, piece, 0.0))
            blk = jnp.concatenate(pieces, axis=0)
            if lag == 0:
                blk = blk + jnp.where(on_diag, d_ref[0], 0.0)
            blks.append(blk.astype(_BF16))
        for s_in in range(t):
            for s_out in range(t):
                mt_scr[s_in * LANES:(s_in + 1) * LANES, s_out * LANES:(s_out + 1) * LANES] = (
                    blks[s_out - s_in] if s_out >= s_in else zero_blk)
        xr_scr[...] = jnp.zeros_like(xr_scr)
        xi_scr[...] = jnp.zeros_like(xi_scr)

    for st in range(t):
        ucat_scr[:, st * LANES:(st + 1) * LANES] = u_ref[0, pl.ds(st, rows, stride=t), :].astype(_BF16)

    s_scr[...] = _dot(ucat_scr[...], mb_scr[...])

    ar = jnp.broadcast_to(ar_ref[0], (batch, SLAB_STATE))
    ai = jnp.broadcast_to(ai_ref[0], (batch, SLAB_STATE))

    xr, xi = xr_scr[...], xi_scr[...]
    for c in range(rows // batch):
        rs = slice(c * batch, (c + 1) * batch)
        sr = s_scr[rs, 0:SLAB_STATE]
        si = s_scr[rs, SLAB_STATE:2 * SLAB_STATE]
        s_scr[rs, 0:SLAB_STATE] = xr
        s_scr[rs, SLAB_STATE:2 * SLAB_STATE] = xi
        xr, xi = ar * xr - ai * xi + sr, ar * xi + ai * xr + si
    xr_scr[...] = xr
    xi_scr[...] = xi

    pair = 2 * LANES
    toeplitz = jnp.concatenate(
        [_dot(ucat_scr[:, 0:c0 + pair], mt_scr[0:c0 + pair, c0:c0 + pair]) for c0 in range(0, t * LANES, pair)],
        axis=1)
    y = toeplitz + lax.dot_general(
        s_scr[...].astype(_BF16), mct_scr[...], (((1,), (1,)), ((), ())), preferred_element_type=_F32)
    for st in range(t):
        y_ref[0, pl.ds(st, rows, stride=t), :] = y[:, st * LANES:(st + 1) * LANES]


def _ssm(u, wc, cct, kt, dd, ar, ai, layer, batch):
    nslab, tok_rows, _ = u.shape
    blk = SSM_CHUNK_BLOCK * batch * SSM_CHUNK
    rows = SSM_CHUNK_BLOCK * batch
    width = SSM_CHUNK * LANES
    io = pl.BlockSpec((1, blk, LANES), lambda q, ci: (q, ci, 0))
    par = lambda *shape: pl.BlockSpec((None, 1) + shape, lambda q, ci: (layer, q) + (0,) * len(shape))
    return pl.pallas_call(
        functools.partial(_ssm_kernel, batch=batch),
        grid=(nslab, tok_rows // blk),
        in_specs=[io, par(SSM_CHUNK, SSM_GROUP, 2 * SLAB_STATE), par(SSM_CHUNK, SSM_GROUP, 2 * SLAB_STATE),
                  par(SLAB_GROUPS, SSM_GROUP, SSM_CHUNK * SSM_GROUP), par(1, LANES),
                  par(1, SLAB_STATE), par(1, SLAB_STATE)],
        out_specs=io,
        out_shape=jax.ShapeDtypeStruct(u.shape, _F32),
        scratch_shapes=[
            pltpu.VMEM((width, 2 * SLAB_STATE), _BF16),
            pltpu.VMEM((width, width), _BF16),
            pltpu.VMEM((width, 2 * SLAB_STATE), _BF16),
            pltpu.VMEM((rows, width), _BF16),
            pltpu.VMEM((rows, 2 * SLAB_STATE), _F32),
            pltpu.VMEM((batch, SLAB_STATE), _F32),
            pltpu.VMEM((batch, SLAB_STATE), _F32),
        ],
        name="ssm",
        compiler_params=pltpu.CompilerParams(
            dimension_semantics=("arbitrary", "arbitrary"), vmem_limit_bytes=VMEM_LIMIT),
    )(u, wc, cct, kt, dd, ar, ai)


def _ssm_operators(a_re, a_im, b_re, b_im, c_re, c_im, d, log_dt):
    t = SSM_CHUNK
    p, hh, nq, ng = SSM_STATE, SSM_GROUP, SSM_SLABS, SLAB_GROUPS
    depth = a_re.shape[0]
    hi = lax.Precision.HIGHEST
    a_re, a_im = a_re.astype(_F32), a_im.astype(_F32)
    dt = jnp.exp(log_dt.astype(_F32))[:, :, None]
    b_re, b_im = b_re.astype(_F32), b_im.astype(_F32)
    c_re, c_im = c_re.astype(_F32), c_im.astype(_F32)

    def powers(ar_, ai_, dt_, n):
        mag, ang = jnp.exp(ar_ * dt_ * n), ai_ * dt_ * n
        return mag * jnp.cos(ang), mag * jnp.sin(ang)

    def zoh(ar_, ai_, dt_):
        lr, li_ = powers(ar_, ai_, dt_, 1.0)
        nr, den = lr - 1.0, ar_ * ar_ + ai_ * ai_
        return (nr * ar_ + li_ * ai_) / den, (li_ * ar_ - nr * ai_) / den

    steps = jnp.arange(t, dtype=_F32)[:, None, None]
    pr, pi = powers(a_re[:, :, None, None], a_im[:, :, None, None], dt[:, :, None, None], steps)
    cr, ci = c_re[:, :, None], c_im[:, :, None]
    lhs = jnp.concatenate([cr * pr - ci * pi, -(cr * pi + ci * pr)], axis=-1)
    f_re, f_im = zoh(a_re, a_im, dt)
    rhs = jnp.concatenate([f_re[..., None] * b_re - f_im[..., None] * b_im,
                           f_re[..., None] * b_im + f_im[..., None] * b_re], axis=-2)
    kt = jnp.einsum("ygck,ygmc->ygkm", rhs, lhs.reshape(depth, -1, t * hh, 2 * p), precision=hi)
    kt = kt.reshape(depth, nq, ng, hh, t * hh)
    dd = d.astype(_F32).reshape(depth, nq, 1, LANES)

    lane = lambda v: v.reshape(depth, nq, 1, ng * p)
    la_re, la_im = lane(a_re), lane(a_im)
    ldt = lane(jnp.broadcast_to(dt, a_re.shape))
    lf_re, lf_im = zoh(la_re, la_im, ldt)
    to_rows = lambda v, axes: v.reshape(depth, nq, ng, *v.shape[2:]).transpose(axes).reshape(depth, nq, hh, ng * p)
    lb_re, lb_im = to_rows(b_re, (0, 1, 4, 2, 3)), to_rows(b_im, (0, 1, 4, 2, 3))
    lc_re, lc_im = to_rows(c_re, (0, 1, 3, 2, 4)), to_rows(c_im, (0, 1, 3, 2, 4))
    bb_re, bb_im = lf_re * lb_re - lf_im * lb_im, lf_re * lb_im + lf_im * lb_re

    rev = (t - 1 - jnp.arange(t, dtype=_F32))[:, None, None]
    rr, ri = powers(la_re[:, :, None], la_im[:, :, None], ldt[:, :, None], rev)
    wc = jnp.concatenate([rr * bb_re[:, :, None] - ri * bb_im[:, :, None],
                          rr * bb_im[:, :, None] + ri * bb_re[:, :, None]], axis=-1)
    fwd = (1.0 + jnp.arange(t, dtype=_F32))[:, None, None]
    fr, fi = powers(la_re[:, :, None], la_im[:, :, None], ldt[:, :, None], fwd)
    cct = jnp.concatenate([lc_re[:, :, None] * fr - lc_im[:, :, None] * fi,
                           -(lc_re[:, :, None] * fi + lc_im[:, :, None] * fr)], axis=-1)
    ar, ai = powers(la_re, la_im, ldt, float(t))
    return wc.astype(_BF16), cct.astype(_BF16), kt, dd, ar, ai


def _mixer_out_kernel(x_ref, attn_ref, g0_ref, acc_ref, g2_ref, y_ref, wao_ref, wglu_ref, wso_ref, wmix_ref,
                      nf_ref, wfi_ref, wfo_ref, nlast_ref, out_ref, *, final):
    tm = x_ref.shape[1]
    part = g0_ref[0].astype(_F32) * _dot(attn_ref[0], wao_ref[...]) + acc_ref[0].astype(_F32)
    y = jnp.concatenate([y_ref[sl].reshape(tm, LANES) for sl in range(SSM_SLABS)], axis=1)
    ys = 0.5 * y * (1.0 + jnp.tanh(math.sqrt(2.0 / math.pi) * (y + 0.044715 * (y * y * y))))
    glu = ys * _sigmoid(_dot(ys.astype(_BF16), wglu_ref[...]))
    y_ssm = _dot(glu.astype(_BF16), wso_ref[...])
    merged = part + g2_ref[0].astype(_F32) * y_ssm
    x1 = x_ref[0] + _dot(merged.astype(_BF16), wmix_ref[...])

    h = _rmsnorm(x1, nf_ref[...]).astype(_BF16)
    gt = _dot(h, wfi_ref[:, 0:FFN_HIDDEN])
    up = _dot(h, wfi_ref[:, FFN_HIDDEN:2 * FFN_HIDDEN])
    act = (gt * _sigmoid(gt) * up).astype(_BF16)
    x2 = x1 + _dot(act, wfo_ref[...])
    if final:
        x2 = _rmsnorm(x2, nlast_ref[...])
    out_ref[0] = x2


def _mixer_out(x, attn, g0, acc, g2, y, wao, wglu, wso, wmix, nf, wfi, wfo, nlast, layer, final, cast_weights):
    b, l, _ = x.shape
    tm = TOKEN_TILE
    nchunk = tm // SSM_CHUNK
    grid = (b, l // tm)
    tok = pl.BlockSpec((1, tm, D_MODEL), lambda bi, li: (bi, li, 0))
    in_specs = [
        tok,
        pl.BlockSpec((1, tm, ATTN_WIDTH), lambda bi, li: (bi, li, 0)),
        tok, tok, tok,
        _ssm_io_spec(nchunk),
        _const_spec((ATTN_WIDTH, D_MODEL)),
        _const_spec((SSM_WIDTH, SSM_WIDTH)),
        _const_spec((SSM_WIDTH, D_MODEL)),
        _const_spec((D_MODEL, D_MODEL)),
        _layer_spec((1, D_MODEL), layer),
        _const_spec((D_MODEL, 2 * FFN_HIDDEN)),
        _const_spec((FFN_HIDDEN, D_MODEL)),
        _const_spec((1, D_MODEL)),
    ]
    cast_in, cast_out, cast_shape = _cast_plan(cast_weights, layer + 1, grid)
    body = functools.partial(_mixer_out_kernel, final=final)
    return pl.pallas_call(
        _with_casts(body, len(in_specs), 1, len(cast_weights)),
        grid=grid, in_specs=in_specs + cast_in, out_specs=[tok] + cast_out,
        out_shape=[jax.ShapeDtypeStruct((b, l, D_MODEL), _F32)] + cast_shape,
        name="mixer_out",
        compiler_params=pltpu.CompilerParams(
            dimension_semantics=("arbitrary", "arbitrary"), vmem_limit_bytes=VMEM_LIMIT),
    )(x, attn, g0, acc, g2, y, wao, wglu, wso, wmix, nf, wfi, wfo, nlast, *cast_weights)


def _rope_tables(seq_len):
    half = ROT_DIM // 2
    pos = jnp.arange(seq_len, dtype=_F32)
    inv_freq = ROPE_THETA ** (-jnp.arange(0, ROT_DIM, 2, dtype=_F32) / ROT_DIM)
    ang = pos[:, None] * inv_freq[None, :]
    cos, sin = jnp.cos(ang), jnp.sin(ang)
    ones = jnp.ones((seq_len, HEAD_DIM - ROT_DIM), _F32)
    zeros_h = jnp.zeros((seq_len, half), _F32)
    zeros_r = jnp.zeros((seq_len, HEAD_DIM - ROT_DIM), _F32)
    cos_h = jnp.concatenate([cos, cos, ones], axis=1)
    sa_h = jnp.concatenate([-sin, zeros_h, zeros_r], axis=1)
    sb_h = jnp.concatenate([zeros_h, sin, zeros_r], axis=1)
    rep = LANES // HEAD_DIM
    return jnp.tile(cos_h, (1, rep)), jnp.tile(sa_h, (1, rep)), jnp.tile(sb_h, (1, rep))


def kernel(x, norm_mix, w_in, b_gate, attn_sinks, w_attn_o, conv_w, w_conv_o, ssm_a_re, ssm_a_im,
           ssm_b_re, ssm_b_im, ssm_c_re, ssm_c_im, ssm_d, ssm_log_dt, w_ssm_glu, w_ssm_o, w_mix_o,
           norm_ffn, w_ffn_in, w_ffn_out, norm_final):
    b, l, d_model = x.shape
    depth = w_in.shape[0]
    assert d_model == D_MODEL and l % TOKEN_TILE == 0 and b % SUBLANES == 0
    assert l % (SSM_CHUNK * SSM_CHUNK_BLOCK) == 0
    assert w_in.shape[2] == IN_COLS and w_ffn_out.shape[1] == FFN_HIDDEN
    cos_t, sa_t, sb_t = _rope_tables(l)
    f32 = lambda p: p.astype(_F32)
    norm_mix, norm_ffn = f32(norm_mix)[:, None, :], f32(norm_ffn)[:, None, :]
    b_gate, conv_w, attn_sinks = f32(b_gate)[:, None, :], f32(conv_w), f32(attn_sinks)
    nlast = f32(norm_final).reshape(1, D_MODEL)
    wc, cct, kt, dd, ar, ai = _ssm_operators(
        ssm_a_re, ssm_a_im, ssm_b_re, ssm_b_im, ssm_c_re, ssm_c_im, ssm_d, ssm_log_dt)
    in_weights = tuple(f32(w) for w in (w_in, w_conv_o))
    out_weights = tuple(f32(w) for w in (w_attn_o, w_ssm_glu, w_ssm_o, w_mix_o, w_ffn_in, w_ffn_out))
    w_mixer_in = [w[0].astype(_BF16) for w in in_weights]
    x = f32(x)
    for i in range(depth):
        last = i == depth - 1
        win, wco = w_mixer_in
        attn, g0, acc, g2, u, *w_mixer_out = _mixer_in(
            x, attn_sinks[i], norm_mix, win, b_gate, cos_t, sa_t, sb_t, conv_w, wco, i, out_weights)
        y = _ssm(u.reshape(SSM_SLABS, l * b, LANES), wc, cct, kt, dd, ar, ai, i, b).reshape(u.shape)
        x, *w_mixer_in = _mixer_out(
            x, attn, g0, acc, g2, y, *w_mixer_out[:4], norm_ffn, *w_mixer_out[4:], nlast, i, last,
            () if last else in_weights)
    return x
```

```python
import functools
import math

import jax
import jax.numpy as jnp
from jax import lax
from jax.experimental import pallas as pl
from jax.experimental.pallas import tpu as pltpu

D_MODEL = 1024
N_Q_HEADS = 8
N_KV_HEADS = 2
HEAD_DIM = 64
Q_GROUP = N_Q_HEADS // N_KV_HEADS
WINDOW = 128
ROPE_THETA = 500000.0
ROT_DIM = HEAD_DIM // 4
ATTN_WIDTH = N_Q_HEADS * HEAD_DIM
KV_WIDTH = N_KV_HEADS * HEAD_DIM
NEG_INF = -1e30
CONV_WIDTH = D_MODEL // 2
CONV_K = 3
SSM_WIDTH = D_MODEL // 2
SSM_GROUP = 16
SSM_GROUPS = SSM_WIDTH // SSM_GROUP
SSM_STATE = 64
GATE_WIDTH = 3 * D_MODEL
FFN_HIDDEN = 2816
NORM_EPS = 1e-6
LOG2_E = math.log2(math.e)

_Q0 = 0
_KV0 = _Q0 + ATTN_WIDTH
_CB0 = _KV0 + 2 * KV_WIDTH
_CC0 = _CB0 + CONV_WIDTH
_CX0 = _CC0 + CONV_WIDTH
_U0 = _CX0 + CONV_WIDTH
_G0 = _U0 + SSM_WIDTH
IN_COLS = _G0 + GATE_WIDTH

LANES = 128
SUBLANES = 8
BF16_ROWS = 2 * SUBLANES
SSM_CHUNK = 8
SSM_SLABS = SSM_WIDTH // LANES
SLAB_GROUPS = LANES // SSM_GROUP
SLAB_STATE = SLAB_GROUPS * SSM_STATE
TOKEN_TILE = 512
SSM_CHUNK_BLOCK = 128
MATMUL_ROWS = 256
VMEM_LIMIT = 56 * 1024 * 1024

_BF16 = jnp.bfloat16
_F32 = jnp.float32


def _dot(a, b):
    m = a.shape[0]
    if m <= MATMUL_ROWS:
        return jnp.dot(a, b, preferred_element_type=_F32)
    return jnp.concatenate(
        [jnp.dot(a[i:i + MATMUL_ROWS], b, preferred_element_type=_F32) for i in range(0, m, MATMUL_ROWS)], axis=0)


def _sigmoid(x):
    return 0.5 * jnp.tanh(0.5 * x) + 0.5


def _rmsnorm(x, w):
    ms = jnp.mean(x * x, axis=-1, keepdims=True)
    return x * lax.rsqrt(ms + NORM_EPS) * w


def _const_spec(shape):
    return pl.BlockSpec(shape, lambda *_: (0,) * len(shape), pipeline_mode=pl.Buffered(1))


def _layer_spec(shape, layer):
    return pl.BlockSpec((None,) + shape, lambda *_: (layer,) + (0,) * len(shape), pipeline_mode=pl.Buffered(1))


def _cast_plan(weights, layer, grid):
    nsteps = grid[0] * grid[1]
    in_specs, out_specs, out_shapes = [], [], []
    for w in weights:
        _, rows, cols = w.shape
        ncol = next(n for n in (1, 2, 4, 8)
                    if (rows * n) % (nsteps * BF16_ROWS) == 0 and cols % (n * LANES) == 0)
        nrow = nsteps // ncol
        block = (rows // nrow, cols // ncol)

        def index(bi, li, nrow=nrow):
            step = bi * grid[1] + li
            return step % nrow, step // nrow

        in_specs.append(pl.BlockSpec((None,) + block, lambda bi, li, index=index: (layer,) + index(bi, li)))
        out_specs.append(pl.BlockSpec(block, index))
        out_shapes.append(jax.ShapeDtypeStruct((rows, cols), _BF16))
    return in_specs, out_specs, out_shapes


def _with_casts(body, n_in, n_out, n_cast):
    def kernel(*refs):
        ins, rest = refs[:n_in], refs[n_in:]
        cast_src, rest = rest[:n_cast], rest[n_cast:]
        outs, rest = rest[:n_out], rest[n_out:]
        cast_dst, scratch = rest[:n_cast], rest[n_cast:]
        body(*ins, *outs, *scratch)
        for src, dst in zip(cast_src, cast_dst):
            dst[...] = src[...].astype(_BF16)
    return kernel


def _ssm_io_spec(nchunk):
    return pl.BlockSpec((SSM_SLABS, nchunk, None, SSM_CHUNK, LANES), lambda bi, li: (0, li, bi, 0, 0))


def _mixer_in_kernel(sinks_ref, x_ref, nw_ref, win_ref, bg_ref, rope_ref, cw_ref,
                     wao_ref, wco_ref, part_ref, g2_ref, u_ref,
                     kv_scr, z_scr, attn_scr, yc_scr, acc_scr, g0_scr):
    tm = x_ref.shape[1]
    li = pl.program_id(1)

    @pl.when(li == 0)
    def _():
        kv_scr[:, 0:WINDOW, :] = jnp.zeros((8, WINDOW, LANES), _BF16)
        z_scr[0:SUBLANES, :] = jnp.zeros((SUBLANES, CONV_WIDTH), _F32)

    x = x_ref[0]
    h = _rmsnorm(x, nw_ref[...]).astype(_BF16)

    cos = rope_ref[:, 0:LANES]
    sa = rope_ref[:, LANES:2 * LANES]
    sb = rope_ref[:, 2 * LANES:3 * LANES]

    def rope(t):
        return (t * cos + pltpu.roll(t, LANES - ROT_DIM // 2, axis=1) * sa
                + pltpu.roll(t, ROT_DIM // 2, axis=1) * sb)

    kv = _dot(h, win_ref[:, _KV0:_KV0 + 2 * KV_WIDTH])
    lane = lax.broadcasted_iota(jnp.int32, (tm, LANES), 1)
    lo = lane < HEAD_DIM
    for kind in range(2):
        t = kv[:, kind * LANES:(kind + 1) * LANES]
        if kind == 0:
            t = rope(t)
        t_sw = pltpu.roll(t, HEAD_DIM, axis=1)
        zero = jnp.zeros_like(t)
        variants = (jnp.where(lo, t, zero), jnp.where(lo, zero, t_sw),
                    jnp.where(lo, t_sw, zero), jnp.where(lo, zero, t))
        for n, val in enumerate(variants):
            kv_scr[kind * 4 + n, WINDOW:WINDOW + tm, :] = val.astype(_BF16)

    q = _dot(h, win_ref[:, _Q0:_Q0 + ATTN_WIDTH])
    row = lax.broadcasted_iota(jnp.int32, (WINDOW, 2 * WINDOW), 0)
    col = lax.broadcasted_iota(jnp.int32, (WINDOW, 2 * WINDOW), 1)
    band_ok = (col > row) & (col <= row + WINDOW)
    first_ok = band_ok & (col >= jnp.where(li > 0, 0, WINDOW))

    def gate(n):
        c0 = _G0 + n * D_MODEL
        return _sigmoid(_dot(h, win_ref[:, c0:c0 + D_MODEL]) + bg_ref[:, n * D_MODEL:(n + 1) * D_MODEL])


    cb = _dot(h, win_ref[:, _CB0:_CB0 + CONV_WIDTH])
    cc = _dot(h, win_ref[:, _CC0:_CC0 + CONV_WIDTH])
    cx = _dot(h, win_ref[:, _CX0:_CX0 + CONV_WIDTH])
    z = cc * cx
    z_scr[SUBLANES:SUBLANES + tm, :] = z
    z1 = z_scr[SUBLANES - 1:SUBLANES - 1 + tm, :]
    z2 = z_scr[SUBLANES - 2:SUBLANES - 2 + tm, :]
    conv = cw_ref[0:1, :] * z2 + cw_ref[1:2, :] * z1 + cw_ref[2:3, :] * z
    yc_scr[...] = (cb * conv).astype(_BF16)
    acc_scr[...] = gate(1) * _dot(yc_scr[...], wco_ref[...])

    g0_scr[...] = gate(0)
    u = _dot(h, win_ref[:, _U0:_U0 + SSM_WIDTH])
    nchunk = tm // SSM_CHUNK
    for sl in range(SSM_SLABS):
        u_ref[sl] = u[:, sl * LANES:(sl + 1) * LANES].reshape(nchunk, SSM_CHUNK, LANES)

    for m in range(N_Q_HEADS // 2):
        qp = (rope(q[:, m * LANES:(m + 1) * LANES]) * (HEAD_DIM ** -0.5 * LOG2_E)).astype(_BF16)
        j = (2 * m) // Q_GROUP
        for r in range(tm // WINDOW):
            qb = qp[r * WINDOW:(r + 1) * WINDOW]
            mask = first_ok if r == 0 else band_ok
            o = None
            for half in range(2):
                sink = sinks_ref[2 * m + half] * LOG2_E
                kb = kv_scr[j * 2 + half, r * WINDOW:(r + 2) * WINDOW, :]
                vb = kv_scr[4 + j * 2 + half, r * WINDOW:(r + 2) * WINDOW, :]
                s = lax.dot_general(qb, kb, (((1,), (1,)), ((), ())), preferred_element_type=_F32)
                s = jnp.where(mask, s, NEG_INF)
                mx = jnp.maximum(jnp.max(s, axis=-1, keepdims=True), sink)
                p = jnp.exp2(s - mx)
                denom = jnp.sum(p, axis=-1, keepdims=True) + jnp.exp2(sink - mx)
                oh = _dot(p.astype(_BF16), vb) * (1.0 / denom)
                o = oh if o is None else o + oh
            attn_scr[r * WINDOW:(r + 1) * WINDOW, m * LANES:(m + 1) * LANES] = o.astype(_BF16)

    g2_ref[0] = gate(2).astype(_BF16)

    y_attn = _dot(attn_scr[...], wao_ref[...])
    part_ref[0] = (g0_scr[...] * y_attn + acc_scr[...]).astype(_BF16)

    kv_scr[:, 0:WINDOW, :] = kv_scr[:, tm:tm + WINDOW, :]
    z_scr[0:SUBLANES, :] = z_scr[tm:tm + SUBLANES, :]


def _mixer_in(x, sinks, nw, win, bg, rope_t, cw, wao, wco, layer, cast_weights):
    b, l, _ = x.shape
    tm = TOKEN_TILE
    nchunk = tm // SSM_CHUNK
    grid = (b, l // tm)
    in_specs = [
        pl.BlockSpec(memory_space=pltpu.SMEM),
        pl.BlockSpec((1, tm, D_MODEL), lambda bi, li: (bi, li, 0)),
        _layer_spec((1, D_MODEL), layer),
        _const_spec((D_MODEL, IN_COLS)),
        _layer_spec((1, GATE_WIDTH), layer),
        pl.BlockSpec((tm, 3 * LANES), lambda bi, li: (li, 0)),
        _layer_spec((CONV_K, CONV_WIDTH), layer),
        _const_spec((ATTN_WIDTH, D_MODEL)),
        _const_spec((CONV_WIDTH, D_MODEL)),
    ]
    out_specs = [
        pl.BlockSpec((1, tm, D_MODEL), lambda bi, li: (bi, li, 0)),
        pl.BlockSpec((1, tm, D_MODEL), lambda bi, li: (bi, li, 0)),
        _ssm_io_spec(nchunk),
    ]
    out_shape = [
        jax.ShapeDtypeStruct((b, l, D_MODEL), _BF16),
        jax.ShapeDtypeStruct((b, l, D_MODEL), _BF16),
        jax.ShapeDtypeStruct((SSM_SLABS, l // SSM_CHUNK, b, SSM_CHUNK, LANES), _F32),
    ]
    scratch = [
        pltpu.VMEM((8, WINDOW + tm, LANES), _BF16),
        pltpu.VMEM((SUBLANES + tm, CONV_WIDTH), _F32),
        pltpu.VMEM((tm, ATTN_WIDTH), _BF16),
        pltpu.VMEM((tm, CONV_WIDTH), _BF16),
        pltpu.VMEM((tm, D_MODEL), _F32),
        pltpu.VMEM((tm, D_MODEL), _F32),
    ]
    cast_in, cast_out, cast_shape = _cast_plan(cast_weights, layer, grid)
    return pl.pallas_call(
        _with_casts(_mixer_in_kernel, len(in_specs), len(out_specs), len(cast_weights)),
        grid=grid, in_specs=in_specs + cast_in, out_specs=out_specs + cast_out,
        out_shape=out_shape + cast_shape, scratch_shapes=scratch, name="mixer_in",
        compiler_params=pltpu.CompilerParams(
            dimension_semantics=("arbitrary", "arbitrary"), vmem_limit_bytes=VMEM_LIMIT),
    )(sinks, x, nw, win, bg, rope_t, cw, wao, wco, *cast_weights)


def _ssm_kernel(u_ref, wc_ref, cct_ref, kt_ref, d_ref, ar_ref, ai_ref, y_ref,
                mb_scr, mt_scr, mct_scr, ucat_scr, s_scr, xr_scr, xi_scr, *, batch):
    t = SSM_CHUNK
    rows = u_ref.shape[1] // t

    @pl.when(pl.program_id(1) == 0)
    def _():
        chan_bits, state_bits = SSM_GROUP.bit_length() - 1, SSM_STATE.bit_length() - 1
        r_grp = lax.broadcasted_iota(jnp.int32, (LANES, 2 * SLAB_STATE), 0) >> chan_bits
        c_grp = (lax.broadcasted_iota(jnp.int32, (LANES, 2 * SLAB_STATE), 1) & (SLAB_STATE - 1)) >> state_bits
        same_state = r_grp == c_grp

        def expand(tbl):
            return jnp.where(same_state, jnp.tile(tbl, (SLAB_GROUPS, 1)), jnp.zeros((), tbl.dtype))

        for st in range(t):
            rs = slice(st * LANES, (st + 1) * LANES)
            mb_scr[rs, :] = expand(wc_ref[0, st]).astype(_BF16)
            mct_scr[rs, :] = expand(cct_ref[0, st]).astype(_BF16)

        lane_grp = lax.broadcasted_iota(jnp.int32, (SSM_GROUP, LANES), 1) >> chan_bits
        on_diag = (lax.broadcasted_iota(jnp.int32, (LANES, LANES), 0)
                   == lax.broadcasted_iota(jnp.int32, (LANES, LANES), 1))
        zero_blk = jnp.zeros((LANES, LANES), _BF16)
        blks = []
        for lag in range(t):
            pieces = []
            for gl in range(SLAB_GROUPS):
                shift = ((gl - lag) * SSM_GROUP) % LANES
                piece = kt_ref[0, gl]
                if shift:
                    piece = pltpu.roll(piece, shift, axis=1)
                pieces.append(jnp.where(lane_grp == gl, piece, 0.0))
            blk = jnp.concatenate(pieces, axis=0)
            if lag == 0:
                blk = blk + jnp.where(on_diag, d_ref[0], 0.0)
            blks.append(blk.astype(_BF16))
        for s_in in range(t):
            for s_out in range(t):
                mt_scr[s_in * LANES:(s_in + 1) * LANES, s_out * LANES:(s_out + 1) * LANES] = (
                    blks[s_out - s_in] if s_out >= s_in else zero_blk)
        xr_scr[...] = jnp.zeros_like(xr_scr)
        xi_scr[...] = jnp.zeros_like(xi_scr)

    for st in range(t):
        ucat_scr[:, st * LANES:(st + 1) * LANES] = u_ref[0, pl.ds(st, rows, stride=t), :].astype(_BF16)

    s_scr[...] = _dot(ucat_scr[...], mb_scr[...])

    ar = jnp.broadcast_to(ar_ref[0], (batch, SLAB_STATE))
    ai = jnp.broadcast_to(ai_ref[0], (batch, SLAB_STATE))

    xr, xi = xr_scr[...], xi_scr[...]
    for c in range(rows // batch):
        rs = slice(c * batch, (c + 1) * batch)
        sr = s_scr[rs, 0:SLAB_STATE]
        si = s_scr[rs, SLAB_STATE:2 * SLAB_STATE]
        s_scr[rs, 0:SLAB_STATE] = xr
        s_scr[rs, SLAB_STATE:2 * SLAB_STATE] = xi
        xr, xi = ar * xr - ai * xi + sr, ar * xi + ai * xr + si
    xr_scr[...] = xr
    xi_scr[...] = xi

    pair = 2 * LANES
    toeplitz = jnp.concatenate(
        [_dot(ucat_scr[:, 0:c0 + pair], mt_scr[0:c0 + pair, c0:c0 + pair]) for c0 in range(0, t * LANES, pair)],
        axis=1)
    y = toeplitz + lax.dot_general(
        s_scr[...].astype(_BF16), mct_scr[...], (((1,), (1,)), ((), ())), preferred_element_type=_F32)
    for st in range(t):
        y_ref[0, pl.ds(st, rows, stride=t), :] = y[:, st * LANES:(st + 1) * LANES]


def _ssm(u, wc, cct, kt, dd, ar, ai, layer, batch):
    nslab, tok_rows, _ = u.shape
    blk = SSM_CHUNK_BLOCK * batch * SSM_CHUNK
    rows = SSM_CHUNK_BLOCK * batch
    width = SSM_CHUNK * LANES
    io = pl.BlockSpec((1, blk, LANES), lambda q, ci: (q, ci, 0))
    par = lambda *shape: pl.BlockSpec((None, 1) + shape, lambda q, ci: (layer, q) + (0,) * len(shape))
    return pl.pallas_call(
        functools.partial(_ssm_kernel, batch=batch),
        grid=(nslab, tok_rows // blk),
        in_specs=[io, par(SSM_CHUNK, SSM_GROUP, 2 * SLAB_STATE), par(SSM_CHUNK, SSM_GROUP, 2 * SLAB_STATE),
                  par(SLAB_GROUPS, SSM_GROUP, SSM_CHUNK * SSM_GROUP), par(1, LANES),
                  par(1, SLAB_STATE), par(1, SLAB_STATE)],
        out_specs=io,
        out_shape=jax.ShapeDtypeStruct(u.shape, _F32),
        scratch_shapes=[
            pltpu.VMEM((width, 2 * SLAB_STATE), _BF16),
            pltpu.VMEM((width, width), _BF16),
            pltpu.VMEM((width, 2 * SLAB_STATE), _BF16),
            pltpu.VMEM((rows, width), _BF16),
            pltpu.VMEM((rows, 2 * SLAB_STATE), _F32),
            pltpu.VMEM((batch, SLAB_STATE), _F32),
            pltpu.VMEM((batch, SLAB_STATE), _F32),
        ],
        name="ssm",
        compiler_params=pltpu.CompilerParams(
            dimension_semantics=("arbitrary", "arbitrary"), vmem_limit_bytes=VMEM_LIMIT),
    )(u, wc, cct, kt, dd, ar, ai)


def _ssm_operators(a_re, a_im, b_re, b_im, c_re, c_im, d, log_dt):
    t = SSM_CHUNK
    p, hh, nq, ng = SSM_STATE, SSM_GROUP, SSM_SLABS, SLAB_GROUPS
    depth = a_re.shape[0]
    hi = lax.Precision.HIGHEST
    a_re, a_im = a_re.astype(_F32), a_im.astype(_F32)
    dt = jnp.exp(log_dt.astype(_F32))[:, :, None]
    b_re, b_im = b_re.astype(_F32), b_im.astype(_F32)
    c_re, c_im = c_re.astype(_F32), c_im.astype(_F32)

    def powers(ar_, ai_, dt_, n):
        mag, ang = jnp.exp(ar_ * dt_ * n), ai_ * dt_ * n
        return mag * jnp.cos(ang), mag * jnp.sin(ang)

    def zoh(ar_, ai_, dt_):
        lr, li_ = powers(ar_, ai_, dt_, 1.0)
        nr, den = lr - 1.0, ar_ * ar_ + ai_ * ai_
        return (nr * ar_ + li_ * ai_) / den, (li_ * ar_ - nr * ai_) / den

    steps = jnp.arange(t, dtype=_F32)[:, None, None]
    pr, pi = powers(a_re[:, :, None, None], a_im[:, :, None, None], dt[:, :, None, None], steps)
    cr, ci = c_re[:, :, None], c_im[:, :, None]
    lhs = jnp.concatenate([cr * pr - ci * pi, -(cr * pi + ci * pr)], axis=-1)
    f_re, f_im = zoh(a_re, a_im, dt)
    rhs = jnp.concatenate([f_re[..., None] * b_re - f_im[..., None] * b_im,
                           f_re[..., None] * b_im + f_im[..., None] * b_re], axis=-2)
    kt = jnp.einsum("ygck,ygmc->ygkm", rhs, lhs.reshape(depth, -1, t * hh, 2 * p), precision=hi)
    kt = kt.reshape(depth, nq, ng, hh, t * hh)
    dd = d.astype(_F32).reshape(depth, nq, 1, LANES)

    lane = lambda v: v.reshape(depth, nq, 1, ng * p)
    la_re, la_im = lane(a_re), lane(a_im)
    ldt = lane(jnp.broadcast_to(dt, a_re.shape))
    lf_re, lf_im = zoh(la_re, la_im, ldt)
    to_rows = lambda v, axes: v.reshape(depth, nq, ng, *v.shape[2:]).transpose(axes).reshape(depth, nq, hh, ng * p)
    lb_re, lb_im = to_rows(b_re, (0, 1, 4, 2, 3)), to_rows(b_im, (0, 1, 4, 2, 3))
    lc_re, lc_im = to_rows(c_re, (0, 1, 3, 2, 4)), to_rows(c_im, (0, 1, 3, 2, 4))
    bb_re, bb_im = lf_re * lb_re - lf_im * lb_im, lf_re * lb_im + lf_im * lb_re

    rev = (t - 1 - jnp.arange(t, dtype=_F32))[:, None, None]
    rr, ri = powers(la_re[:, :, None], la_im[:, :, None], ldt[:, :, None], rev)
    wc = jnp.concatenate([rr * bb_re[:, :, None] - ri * bb_im[:, :, None],
                          rr * bb_im[:, :, None] + ri * bb_re[:, :, None]], axis=-1)
    fwd = (1.0 + jnp.arange(t, dtype=_F32))[:, None, None]
    fr, fi = powers(la_re[:, :, None], la_im[:, :, None], ldt[:, :, None], fwd)
    cct = jnp.concatenate([lc_re[:, :, None] * fr - lc_im[:, :, None] * fi,
                           -(lc_re[:, :, None] * fi + lc_im[:, :, None] * fr)], axis=-1)
    ar, ai = powers(la_re, la_im, ldt, float(t))
    return wc.astype(_BF16), cct.astype(_BF16), kt, dd, ar, ai


def _mixer_out_kernel(x_ref, part_ref, g2_ref, y_ref, wglu_ref, wso_ref, wmix_ref, nf_ref,
                      wfi_ref, wfo_ref, nlast_ref, out_ref, *, final):
    tm = x_ref.shape[1]
    y = jnp.concatenate([y_ref[sl].reshape(tm, LANES) for sl in range(SSM_SLABS)], axis=1)
    ys = 0.5 * y * (1.0 + jnp.tanh(math.sqrt(2.0 / math.pi) * (y + 0.044715 * (y * y * y))))
    glu = ys * _sigmoid(_dot(ys.astype(_BF16), wglu_ref[...]))
    y_ssm = _dot(glu.astype(_BF16), wso_ref[...])
    merged = part_ref[0].astype(_F32) + g2_ref[0].astype(_F32) * y_ssm
    x1 = x_ref[0] + _dot(merged.astype(_BF16), wmix_ref[...])

    h = _rmsnorm(x1, nf_ref[...]).astype(_BF16)
    gt = _dot(h, wfi_ref[:, 0:FFN_HIDDEN])
    up = _dot(h, wfi_ref[:, FFN_HIDDEN:2 * FFN_HIDDEN])
    act = (gt * _sigmoid(gt) * up).astype(_BF16)
    x2 = x1 + _dot(act, wfo_ref[...])
    if final:
        x2 = _rmsnorm(x2, nlast_ref[...])
    out_ref[0] = x2


def _mixer_out(x, part, g2, y, wglu, wso, wmix, nf, wfi, wfo, nlast, layer, final, cast_weights):
    b, l, _ = x.shape
    tm = TOKEN_TILE
    nchunk = tm // SSM_CHUNK
    grid = (b, l // tm)
    tok = pl.BlockSpec((1, tm, D_MODEL), lambda bi, li: (bi, li, 0))
    in_specs = [
        tok, tok, tok,
        _ssm_io_spec(nchunk),
        _const_spec((SSM_WIDTH, SSM_WIDTH)),
        _const_spec((SSM_WIDTH, D_MODEL)),
        _const_spec((D_MODEL, D_MODEL)),
        _layer_spec((1, D_MODEL), layer),
        _const_spec((D_MODEL, 2 * FFN_HIDDEN)),
        _const_spec((FFN_HIDDEN, D_MODEL)),
        _const_spec((1, D_MODEL)),
    ]
    cast_in, cast_out, cast_shape = _cast_plan(cast_weights, layer + 1, grid)
    body = functools.partial(_mixer_out_kernel, final=final)
    return pl.pallas_call(
        _with_casts(body, len(in_specs), 1, len(cast_weights)),
        grid=grid, in_specs=in_specs + cast_in, out_specs=[tok] + cast_out,
        out_shape=[jax.ShapeDtypeStruct((b, l, D_MODEL), _F32)] + cast_shape,
        name="mixer_out",
        compiler_params=pltpu.CompilerParams(
            dimension_semantics=("arbitrary", "arbitrary"), vmem_limit_bytes=VMEM_LIMIT),
    )(x, part, g2, y, wglu, wso, wmix, nf, wfi, wfo, nlast, *cast_weights)


def _rope_tables(seq_len):
    half = ROT_DIM // 2
    pos = jnp.arange(seq_len, dtype=_F32)
    inv_freq = ROPE_THETA ** (-jnp.arange(0, ROT_DIM, 2, dtype=_F32) / ROT_DIM)
    ang = pos[:, None] * inv_freq[None, :]
    cos, sin = jnp.cos(ang), jnp.sin(ang)
    ones = jnp.ones((seq_len, HEAD_DIM - ROT_DIM), _F32)
    zeros_h = jnp.zeros((seq_len, half), _F32)
    zeros_r = jnp.zeros((seq_len, HEAD_DIM - ROT_DIM), _F32)
    cos_h = jnp.concatenate([cos, cos, ones], axis=1)
    sa_h = jnp.concatenate([-sin, zeros_h, zeros_r], axis=1)
    sb_h = jnp.concatenate([zeros_h, sin, zeros_r], axis=1)
    rep = LANES // HEAD_DIM
    return jnp.tile(cos_h, (1, rep)), jnp.tile(sa_h, (1, rep)), jnp.tile(sb_h, (1, rep))


def kernel(x, norm_mix, w_in, b_gate, attn_sinks, w_attn_o, conv_w, w_conv_o, ssm_a_re, ssm_a_im,
           ssm_b_re, ssm_b_im, ssm_c_re, ssm_c_im, ssm_d, ssm_log_dt, w_ssm_glu, w_ssm_o, w_mix_o,
           norm_ffn, w_ffn_in, w_ffn_out, norm_final):
    b, l, d_model = x.shape
    depth = w_in.shape[0]
    assert d_model == D_MODEL and l % TOKEN_TILE == 0 and b % SUBLANES == 0
    assert l % (SSM_CHUNK * SSM_CHUNK_BLOCK) == 0
    assert w_in.shape[2] == IN_COLS and w_ffn_out.shape[1] == FFN_HIDDEN
    rope_t = jnp.concatenate(_rope_tables(l), axis=1)
    f32 = lambda p: p.astype(_F32)
    norm_mix, norm_ffn = f32(norm_mix)[:, None, :], f32(norm_ffn)[:, None, :]
    b_gate, conv_w, attn_sinks = f32(b_gate)[:, None, :], f32(conv_w), f32(attn_sinks)
    nlast = f32(norm_final).reshape(1, D_MODEL)
    wc, cct, kt, dd, ar, ai = _ssm_operators(
        ssm_a_re, ssm_a_im, ssm_b_re, ssm_b_im, ssm_c_re, ssm_c_im, ssm_d, ssm_log_dt)
    in_weights = tuple(f32(w) for w in (w_in, w_attn_o, w_conv_o))
    out_weights = tuple(f32(w) for w in (w_ssm_glu, w_ssm_o, w_mix_o, w_ffn_in, w_ffn_out))
    w_mixer_in = [w[0].astype(_BF16) for w in in_weights]
    x = f32(x)
    for i in range(depth):
        last = i == depth - 1
        win, wao, wco = w_mixer_in
        part, g2, u, *w_mixer_out = _mixer_in(
            x, attn_sinks[i], norm_mix, win, b_gate, rope_t, conv_w, wao, wco, i, out_weights)
        y = _ssm(u.reshape(SSM_SLABS, l * b, LANES), wc, cct, kt, dd, ar, ai, i, b).reshape(u.shape)
        x, *w_mixer_in = _mixer_out(
            x, part, g2, y, *w_mixer_out[:3], norm_ffn, *w_mixer_out[3:], nlast, i, last,
            () if last else in_weights)
    return x
```
